```python
import math
import jax, jax.numpy as jnp
from jax import lax
import numpy as np

D_MODEL = 2048
BATCH = 4
SEQ = 2048
DEPTH = 4
DEC_BATCH = 128
DEC_SEQ = 4
PAST_LEN = 8192
PAGE_SIZE = 128

N_META = 16
N_A_LAYERS = DEPTH // 2
N_B_LAYERS = DEPTH - N_A_LAYERS
EPS = 1e-6

D_INNER = 2 * D_MODEL
SSM_HEAD_DIM = 64
SSM_HEADS = D_INNER // SSM_HEAD_DIM
SSM_GROUPS = 8
SSM_HPG = SSM_HEADS // SSM_GROUPS
SSM_STATE = 128
CONV_WIDTH = 4
GN = SSM_GROUPS * SSM_STATE
CONV_DIM = D_INNER + 2 * GN
A_IN_DIM = D_INNER + CONV_DIM + SSM_HEADS
SSD_CHUNK = 128

MLA_HEADS = 16
Q_LORA = 512
KV_LORA = 512
NOPE_DIM = 128
ROPE_DIM = 64
QK_DIM = NOPE_DIM + ROPE_DIM
V_DIM = 128
MLA_WIDTH = MLA_HEADS * V_DIM
B_IN_DIM = Q_LORA + MLA_WIDTH
KV_ROW = KV_LORA + ROPE_DIM
ROPE_BASE = 10000.0
Q_BLOCK = 128

kernel_name = "hybrid_ssd_yoco_mla_step"

F32 = jnp.float32


def rmsnorm(x, g):
    xf = x.astype(F32)
    y = xf * lax.rsqrt(jnp.mean(xf * xf, -1, keepdims=True) + EPS)
    return (y * g.astype(F32)).astype(x.dtype)


def rope_cos_sin(pos):
    inv = 1.0 / (ROPE_BASE ** (jnp.arange(0, ROPE_DIM, 2, dtype=F32) / ROPE_DIM))
    f = pos.astype(F32)[:, None] * inv[None, :]
    emb = jnp.concatenate([f, f], -1)
    return jnp.cos(emb), jnp.sin(emb)


def apply_rope(x, cos, sin):
    xf = x.astype(F32)
    h = ROPE_DIM // 2
    rot = jnp.concatenate([-xf[..., h:], xf[..., :h]], -1)
    return (xf * cos + rot * sin).astype(x.dtype)


def causal_conv(xbc, prev, w, b):
    L = xbc.shape[1]
    full = jnp.concatenate([prev.astype(xbc.dtype), xbc], axis=1)
    out = b + full[:, 0:L] * w[0]
    for k in range(1, CONV_WIDTH):
        out = out + full[:, k:k + L] * w[k]
    return jax.nn.silu(out), full[:, L:]


def ssd_chunk_scan(x, dt, A, Bm, Cm, s0, chunk):
    bt, L = x.shape[:2]
    nc = L // chunk
    G, E, P, N = SSM_GROUPS, SSM_HPG, SSM_HEAD_DIM, SSM_STATE

    def to_chunks(a, tail):
        return a.astype(F32).reshape((bt, nc, chunk) + tail).swapaxes(0, 1)

    xc = to_chunks(x, (G, E, P))
    dtc = to_chunks(dt, (G, E))
    Bc = to_chunks(Bm, (G, N))
    Cc = to_chunks(Cm, (G, N))
    Ag = A.reshape(G, E)
    causal = jnp.tril(jnp.ones((chunk, chunk), bool))[None, :, :, None, None]

    def step(s, inp):
        xk, dtk, Bk, Ck = inp
        cum = jnp.cumsum(dtk * Ag, axis=1)
        seg = cum[:, :, None] - cum[:, None, :]
        decay = jnp.exp(jnp.where(causal, seg, -jnp.inf))
        cb = jnp.einsum('bign,bjgn->bijg', Ck, Bk)
        xdt = xk * dtk[..., None]
        y = jnp.einsum('bijge,bjgep->bigep', decay * cb[..., None], xdt)
        y = y + jnp.einsum('bign,bgepn->bigep', Ck, s) * jnp.exp(cum)[..., None]
        last = cum[:, -1]
        w_end = jnp.exp(last[:, None] - cum)
        s_new = s * jnp.exp(last)[..., None, None] + jnp.einsum('bjge,bjgep,bjgn->bgepn', w_end, xdt, Bk)
        return s_new, y

    s_fin, ys = lax.scan(step, s0.astype(F32).reshape(bt, G, E, P, N), (xc, dtc, Bc, Cc))
    y = ys.swapaxes(0, 1).reshape(bt, L, SSM_HEADS, P)
    return y, s_fin.reshape(bt, SSM_HEADS, P, N)


def mamba_layer(x, conv_prev, s_prev, norm_g, w_in, conv_w, conv_b, dt_bias, A_log, D_skip, gate_norm, w_out, segments):
    bt, L = x.shape[:2]
    h = rmsnorm(x, norm_g)
    proj = h @ w_in
    z = proj[..., :D_INNER]
    xbc = proj[..., D_INNER:D_INNER + CONV_DIM]
    dt_raw = proj[..., D_INNER + CONV_DIM:]
    xbc, conv_new = causal_conv(xbc, conv_prev, conv_w, conv_b)
    xs = xbc[..., :D_INNER].reshape(bt, L, SSM_HEADS, SSM_HEAD_DIM)
    Bm = xbc[..., D_INNER:D_INNER + GN].reshape(bt, L, SSM_GROUPS, SSM_STATE)
    Cm = xbc[..., D_INNER + GN:].reshape(bt, L, SSM_GROUPS, SSM_STATE)
    dt = jax.nn.softplus(dt_raw.astype(F32) + dt_bias.astype(F32))
    A = -jnp.exp(A_log.astype(F32))
    s = s_prev.astype(F32)
    ys = []
    start = 0
    for length, chunk in segments:
        sl = slice(start, start + length)
        y_seg, s = ssd_chunk_scan(xs[:, sl], dt[:, sl], A, Bm[:, sl], Cm[:, sl], s, chunk)
        ys.append(y_seg)
        start += length
    y = jnp.concatenate(ys, axis=1) if len(ys) > 1 else ys[0]
    y = y + xs.astype(F32) * D_skip.astype(F32)[:, None]
    y = y.reshape(bt, L, D_INNER) * jax.nn.silu(z.astype(F32))
    yg = y.reshape(bt, L, SSM_GROUPS, D_INNER // SSM_GROUPS)
    yg = yg * lax.rsqrt(jnp.mean(yg * yg, -1, keepdims=True) + EPS)
    y = (yg.reshape(bt, L, D_INNER) * gate_norm.astype(F32)).astype(x.dtype)
    return x + y @ w_out, conv_new, s.astype(s_prev.dtype)


def mla_kv_side(x, kv_norm, w_kv_a, kv_a_norm, cos, sin):
    a = rmsnorm(x, kv_norm) @ w_kv_a
    c = rmsnorm(a[..., :KV_LORA], kv_a_norm)
    kr = apply_rope(a[..., KV_LORA:], cos, sin)
    return jnp.concatenate([c, kr], -1)


def key_inv_rms(rows, w_uk):
    rf = rows.astype(F32)
    kn = jnp.einsum('...kl,lhd->...khd', rf[..., :KV_LORA], w_uk.astype(F32))
    ss = jnp.sum(kn * kn, -1) + jnp.sum(rf[..., KV_LORA:] ** 2, -1)[..., None]
    return lax.rsqrt(ss / QK_DIM + EPS)


def past_key_inv_rms(cache_kv, page_table, w_uk):
    past = page_table.shape[1] * PAGE_SIZE
    return lax.map(lambda pages: key_inv_rms(cache_kv[pages].reshape(past, KV_ROW), w_uk), page_table)


def mla_query(x, norm_g, w_in, q_a_norm, w_q, q_norm, k_norm, cos, sin):
    proj = rmsnorm(x, norm_g) @ w_in
    qa = rmsnorm(proj[..., :Q_LORA], q_a_norm)
    gate = proj[..., Q_LORA:]
    q = (qa @ w_q).reshape(x.shape[:2] + (MLA_HEADS, QK_DIM))
    q = jnp.concatenate([q[..., :NOPE_DIM], apply_rope(q[..., NOPE_DIM:], cos[:, None], sin[:, None])], -1)
    qf = q.astype(F32)
    qf = qf * lax.rsqrt(jnp.mean(qf * qf, -1, keepdims=True) + EPS)
    qf = qf * (q_norm.astype(F32) * k_norm.astype(F32) / math.sqrt(QK_DIM))
    return qf.astype(x.dtype), gate


def mla_attend(q_eff, rows, r, mask, w_uk, w_uv):
    q_lat = jnp.einsum('...qhd,lhd->...qhl', q_eff[..., :NOPE_DIM], w_uk)
    s = (jnp.einsum('...qhl,...kl->...hqk', q_lat, rows[..., :KV_LORA])
         + jnp.einsum('...qhd,...kd->...hqk', q_eff[..., NOPE_DIM:], rows[..., KV_LORA:])).astype(F32)
    s = s * jnp.swapaxes(r, -1, -2)[..., :, None, :]
    s = jnp.where(mask, s, -jnp.inf)
    p = jax.nn.softmax(s, axis=-1).astype(rows.dtype)
    o_lat = jnp.einsum('...hqk,...kl->...qhl', p, rows[..., :KV_LORA])
    return jnp.einsum('...qhl,lhd->...qhd', o_lat, w_uv)


def mla_prompt_attention(q_eff, rows, r, w_uk, w_uv):
    bt, T = q_eff.shape[:2]
    nb = -(-T // Q_BLOCK)
    pad = nb * Q_BLOCK - T
    qp = jnp.pad(q_eff, ((0, 0), (0, pad), (0, 0), (0, 0)))
    qb = qp.reshape(bt, nb, Q_BLOCK, MLA_HEADS, QK_DIM).swapaxes(0, 1)
    starts = jnp.arange(nb) * Q_BLOCK
    kpos = jnp.arange(T)

    def block(args):
        qblk, st = args
        qpos = st + jnp.arange(Q_BLOCK)
        mask = kpos[None, :] <= qpos[:, None]
        return mla_attend(qblk, rows, r, mask, w_uk, w_uv)

    o = lax.map(block, (qb, starts))
    return o.swapaxes(0, 1).reshape(bt, nb * Q_BLOCK, MLA_HEADS, V_DIM)[:, :T]


def mla_sample_attention(q_eff, new_rows, new_r, past_r, cache_kv, page_table, w_uk, w_uv):
    S = q_eff.shape[1]
    past = page_table.shape[1] * PAGE_SIZE
    kidx = jnp.arange(past + S)
    qidx = past + jnp.arange(S)
    mask = kidx[None, :] <= qidx[:, None]

    def one(args):
        q, nrow, nr, pr, pages = args
        prow = cache_kv[pages].reshape(past, KV_ROW).astype(nrow.dtype)
        rows = jnp.concatenate([prow, nrow], 0)
        rr = jnp.concatenate([pr, nr], 0)
        return mla_attend(q, rows, rr, mask, w_uk, w_uv)

    return lax.map(one, (q_eff, new_rows, new_r, past_r, page_table))


def mla_out(x, o, gate, w_out):
    o = o.reshape(x.shape[:2] + (MLA_WIDTH,)) * jax.nn.silu(gate)
    return x + o @ w_out


def setup_inputs(seed: int = 0) -> dict:
    key = jax.random.key(seed)
    k = iter(jax.random.split(key, 40))

    def nrm(shape, scale):
        return jax.random.normal(next(k), shape, F32) * scale

    def gain(shape):
        return 1.0 + nrm(shape, 0.02)

    n_pages = PAST_LEN // PAGE_SIZE
    n_pool = (DEC_BATCH * n_pages * 5) // 4
    page_table = jax.random.permutation(next(k), n_pool)[:DEC_BATCH * n_pages].reshape(DEC_BATCH, n_pages).astype(jnp.int32)
    u = jax.random.uniform(next(k), (N_A_LAYERS, SSM_HEADS), F32)
    dt0 = jnp.exp(u * (math.log(0.1) - math.log(1e-3)) + math.log(1e-3))
    a_dt_bias = dt0 + jnp.log(-jnp.expm1(-dt0))
    a_A_log = jnp.log(jax.random.uniform(next(k), (N_A_LAYERS, SSM_HEADS), F32, 1.0, 16.0))
    return {
        "x_prompt": nrm((BATCH, SEQ, D_MODEL), 1.0),
        "x_sample": nrm((DEC_BATCH, DEC_SEQ, D_MODEL), 1.0),
        "state_ssm": nrm((N_A_LAYERS, DEC_BATCH, SSM_HEADS, SSM_HEAD_DIM, SSM_STATE), 0.1),
        "state_conv": nrm((N_A_LAYERS, DEC_BATCH, CONV_WIDTH - 1, CONV_DIM), 1.0),
        "cache_kv": nrm((n_pool, PAGE_SIZE, KV_ROW), 1.0),
        "page_table": page_table,
        "meta_tokens": nrm((N_META, D_MODEL), 1.0),
        "a_norm": gain((N_A_LAYERS, D_MODEL)),
        "a_w_in": nrm((N_A_LAYERS, D_MODEL, A_IN_DIM), D_MODEL ** -0.5),
        "a_conv_w": nrm((N_A_LAYERS, CONV_WIDTH, CONV_DIM), 0.5),
        "a_conv_b": nrm((N_A_LAYERS, CONV_DIM), 0.02),
        "a_dt_bias": a_dt_bias,
        "a_A_log": a_A_log,
        "a_D": gain((N_A_LAYERS, SSM_HEADS)),
        "a_gate_norm": gain((N_A_LAYERS, D_INNER)),
        "a_w_out": nrm((N_A_LAYERS, D_INNER, D_MODEL), D_INNER ** -0.5),
        "kv_norm": gain((D_MODEL,)),
        "w_kv_a": nrm((D_MODEL, KV_ROW), D_MODEL ** -0.5),
        "kv_a_norm": gain((KV_LORA,)),
        "w_uk": nrm((KV_LORA, MLA_HEADS, NOPE_DIM), KV_LORA ** -0.5),
        "w_uv": nrm((KV_LORA, MLA_HEADS, V_DIM), KV_LORA ** -0.5),
        "k_norm": gain((QK_DIM,)),
        "b_norm": gain((N_B_LAYERS, D_MODEL)),
        "b_w_in": nrm((N_B_LAYERS, D_MODEL, B_IN_DIM), D_MODEL ** -0.5),
        "b_q_a_norm": gain((N_B_LAYERS, Q_LORA)),
        "b_w_q": nrm((N_B_LAYERS, Q_LORA, MLA_HEADS * QK_DIM), Q_LORA ** -0.5),
        "b_q_norm": gain((N_B_LAYERS, QK_DIM)),
        "b_w_out": nrm((N_B_LAYERS, MLA_WIDTH, D_MODEL), MLA_WIDTH ** -0.5),
    }


def reference(x_prompt, x_sample, state_ssm, state_conv, cache_kv, page_table, meta_tokens,
              a_norm, a_w_in, a_conv_w, a_conv_b, a_dt_bias, a_A_log, a_D, a_gate_norm, a_w_out,
              kv_norm, w_kv_a, kv_a_norm, w_uk, w_uv, k_norm,
              b_norm, b_w_in, b_q_a_norm, b_w_q, b_q_norm, b_w_out):
    bt, L = x_prompt.shape[:2]
    T = L + N_META
    S = x_sample.shape[1]
    past = page_table.shape[1] * PAGE_SIZE
    xp = jnp.concatenate([jnp.broadcast_to(meta_tokens.astype(x_prompt.dtype)[None], (bt, N_META, D_MODEL)), x_prompt], 1)
    xs = x_sample
    segs_p = ((N_META, N_META), (L, SSD_CHUNK))
    segs_s = ((S, S),)
    conv0 = jnp.zeros((bt, CONV_WIDTH - 1, CONV_DIM), xp.dtype)
    ssm0 = jnp.zeros((bt, SSM_HEADS, SSM_HEAD_DIM, SSM_STATE), F32)
    ssm_p, conv_p, ssm_s, conv_s = [], [], [], []
    for i in range(N_A_LAYERS):
        w = (a_norm[i], a_w_in[i], a_conv_w[i], a_conv_b[i], a_dt_bias[i], a_A_log[i], a_D[i], a_gate_norm[i], a_w_out[i])
        xp, cp, sp = mamba_layer(xp, conv0, ssm0, *w, segs_p)
        xs, cs, ss = mamba_layer(xs, state_conv[i], state_ssm[i], *w, segs_s)
        ssm_p.append(sp)
        conv_p.append(cp)
        ssm_s.append(ss)
        conv_s.append(cs)

    cos_p, sin_p = rope_cos_sin(jnp.arange(T))
    cos_s, sin_s = rope_cos_sin(past + jnp.arange(S))
    kv_p = mla_kv_side(xp, kv_norm, w_kv_a, kv_a_norm, cos_p, sin_p)
    kv_s = mla_kv_side(xs, kv_norm, w_kv_a, kv_a_norm, cos_s, sin_s)
    r_p = key_inv_rms(kv_p, w_uk)
    r_s = key_inv_rms(kv_s, w_uk)
    r_past = past_key_inv_rms(cache_kv, page_table, w_uk)

    for j in range(N_B_LAYERS):
        qp, gp = mla_query(xp, b_norm[j], b_w_in[j], b_q_a_norm[j], b_w_q[j], b_q_norm[j], k_norm, cos_p, sin_p)
        op = mla_prompt_attention(qp, kv_p, r_p, w_uk, w_uv)
        xp = mla_out(xp, op, gp, b_w_out[j])
        qs, gs = mla_query(xs, b_norm[j], b_w_in[j], b_q_a_norm[j], b_w_q[j], b_q_norm[j], k_norm, cos_s, sin_s)
        osmp = mla_sample_attention(qs, kv_s, r_s, r_past, cache_kv, page_table, w_uk, w_uv)
        xs = mla_out(xs, osmp, gs, b_w_out[j])

    y_prompt = xp[:, N_META:]
    return (y_prompt, xs, jnp.stack(ssm_p), jnp.stack(conv_p), kv_p, jnp.stack(ssm_s), jnp.stack(conv_s), kv_s)
```

```python
import functools
import math

import jax
import jax.numpy as jnp
from jax import lax
from jax.experimental import pallas as pl
from jax.experimental.pallas import tpu as pltpu

F32 = jnp.float32
BF16 = jnp.bfloat16
EPS = 1e-6
ROPE_BASE = 10000.0
LANES = 128
SUBLANES = 8
CHUNK = 128
NEG = -1e30
VMEM_LIMIT = 48 * 1024 * 1024


def _cp(*sem):
    return pltpu.CompilerParams(dimension_semantics=sem, vmem_limit_bytes=VMEM_LIMIT)


def _dot(a, b):
    return jnp.dot(a, b, preferred_element_type=F32)


def _dot_nt(a, b):
    return lax.dot_general(a, b, (((1,), (1,)), ((), ())), preferred_element_type=F32)


def _split3(a):
    h = a.astype(BF16)
    r = a - h.astype(F32)
    m = r.astype(BF16)
    l = (r - m.astype(F32)).astype(BF16)
    return h, m, l


def _silu(v):
    return v * jax.nn.sigmoid(v)


def _pick(n, target):
    best = None
    for t in range(8, min(n, target) + 1, 8):
        if n % t == 0:
            best = t
    assert best is not None, (n, target)
    return best


def _norm_matmul_kernel(x_ref, g_ref, w_ref, o_ref, xn_ref):
    @pl.when(pl.program_id(1) == 0)
    def _():
        x = x_ref[...]
        ms = jnp.mean(x * x, axis=-1, keepdims=True)
        xn_ref[...] = (x * lax.rsqrt(ms + EPS) * g_ref[...]).astype(BF16)

    o_ref[...] = _dot(xn_ref[...], w_ref[...])


def norm_matmul(x, g, w, tm, tn, name):
    M, K = x.shape
    N = w.shape[1]
    return pl.pallas_call(
        _norm_matmul_kernel,
        grid=(M // tm, N // tn),
        in_specs=[pl.BlockSpec((tm, K), lambda i, j: (i, 0)),
                  pl.BlockSpec((1, K), lambda i, j: (0, 0)),
                  pl.BlockSpec((K, tn), lambda i, j: (0, j))],
        out_specs=pl.BlockSpec((tm, tn), lambda i, j: (i, j)),
        out_shape=jax.ShapeDtypeStruct((M, N), F32),
        scratch_shapes=[pltpu.VMEM((tm, K), BF16)],
        compiler_params=_cp("parallel", "arbitrary"),
        name=name,
    )(x, g.reshape(1, K), w)


def _dt_kernel(x_ref, g_ref, wh_ref, wl_ref, b_ref, o_ref):
    x = x_ref[...]
    ms = jnp.mean(x * x, axis=-1, keepdims=True)
    xn = x * lax.rsqrt(ms + EPS) * g_ref[...]
    xh = xn.astype(BF16)
    xl = (xn - xh.astype(F32)).astype(BF16)
    v = _dot(xh, wh_ref[...]) + _dot(xl, wh_ref[...]) + _dot(xh, wl_ref[...]) + b_ref[...]
    o_ref[...] = jnp.maximum(v, 0.0) + jnp.log(1.0 + jnp.exp(-jnp.abs(v)))


def dt_proj(x, g, w_dt, bias, tm, name):
    M, K = x.shape
    H = w_dt.shape[1]
    Hp = -(-H // LANES) * LANES
    wp = jnp.pad(w_dt, ((0, 0), (0, Hp - H)))
    wh = wp.astype(BF16)
    wl = (wp - wh.astype(F32)).astype(BF16)
    bp = jnp.pad(bias.astype(F32), (0, Hp - H)).reshape(1, Hp)
    return pl.pallas_call(
        _dt_kernel,
        grid=(M // tm,),
        in_specs=[pl.BlockSpec((tm, K), lambda i: (i, 0)),
                  pl.BlockSpec((1, K), lambda i: (0, 0)),
                  pl.BlockSpec((K, Hp), lambda i: (0, 0)),
                  pl.BlockSpec((K, Hp), lambda i: (0, 0)),
                  pl.BlockSpec((1, Hp), lambda i: (0, 0))],
        out_specs=pl.BlockSpec((tm, Hp), lambda i: (i, 0)),
        out_shape=jax.ShapeDtypeStruct((M, Hp), F32),
        compiler_params=_cp("parallel"),
        name=name,
    )(x, g.reshape(1, K), wh, wl, bp)


def _mmres_kernel(*refs, gated, nk):
    if gated:
        a_ref, gate_ref, w_ref, r_ref, o_ref, acc = refs
    else:
        a_ref, w_ref, r_ref, o_ref, acc = refs
    k = pl.program_id(2)

    @pl.when(k == 0)
    def _():
        acc[...] = jnp.zeros_like(acc)

    a = a_ref[...]
    if gated:
        a = (a * _silu(gate_ref[...])).astype(BF16)
    acc[...] += _dot(a, w_ref[...])

    @pl.when(k == nk - 1)
    def _():
        o_ref[...] = r_ref[...] + acc[...]


def matmul_res(a, w, res, tm, tn, tk, name, gate=None, gate_col0=0):
    M, K = a.shape
    N = w.shape[1]
    nk = K // tk
    gated = gate is not None
    in_specs = [pl.BlockSpec((tm, tk), lambda i, j, k: (i, k))]
    args = [a]
    if gated:
        goff = gate_col0 // tk
        assert goff * tk == gate_col0
        in_specs.append(pl.BlockSpec((tm, tk), lambda i, j, k: (i, goff + k)))
        args.append(gate)
    in_specs += [pl.BlockSpec((tk, tn), lambda i, j, k: (k, j)),
                 pl.BlockSpec((tm, tn), lambda i, j, k: (i, j))]
    args += [w, res]
    return pl.pallas_call(
        functools.partial(_mmres_kernel, gated=gated, nk=nk),
        grid=(M // tm, N // tn, nk),
        in_specs=in_specs,
        out_specs=pl.BlockSpec((tm, tn), lambda i, j, k: (i, j)),
        out_shape=jax.ShapeDtypeStruct((M, N), F32),
        scratch_shapes=[pltpu.VMEM((tm, tn), F32)],
        compiler_params=_cp("parallel", "parallel", "arbitrary"),
        name=name,
    )(*args)


def _conv_kernel(x_ref, w_ref, b_ref, o_ref, *, width, zero_head):
    x = x_ref[0]
    rid = lax.broadcasted_iota(jnp.int32, x.shape, 0)
    acc = b_ref[...] + x * w_ref[width - 1:width, :]
    for k in range(width - 1):
        s = width - 1 - k
        xs = pltpu.roll(x, s, 0)
        if zero_head:
            xs = jnp.where(rid < s, 0.0, xs)
        acc = acc + xs * w_ref[k:k + 1, :]
    o_ref[0] = _silu(acc)


def conv_silu(inp, col0, w, b, tc, zero_head, name):
    Bt, T, _ = inp.shape
    W, C = w.shape
    off = col0 // tc
    assert off * tc == col0 and C % tc == 0
    return pl.pallas_call(
        functools.partial(_conv_kernel, width=W, zero_head=zero_head),
        grid=(Bt, C // tc),
        in_specs=[pl.BlockSpec((1, T, tc), lambda b, j: (b, 0, off + j)),
                  pl.BlockSpec((W, tc), lambda b, j: (0, j)),
                  pl.BlockSpec((1, tc), lambda b, j: (0, j))],
        out_specs=pl.BlockSpec((1, T, tc), lambda b, j: (b, 0, j)),
        out_shape=jax.ShapeDtypeStruct((Bt, T, C), F32),
        compiler_params=_cp("parallel", "parallel"),
        name=name,
    )(inp, w, b.reshape(1, C))


def _ssd_kernel(*refs, seg, lo, hi, tile, carry, hpg, hd, nchunk):
    if carry:
        (z_ref, x_ref, b_ref, c_ref, dtT_ref, acol_ref, aw_ref, dw_ref, gw_ref,
         y_ref, st_ref, state) = refs
    else:
        (z_ref, x_ref, b_ref, c_ref, dtT_ref, acol_ref, aw_ref, dw_ref, gw_ref,
         ys_ref, y_ref) = refs
    ci = pl.program_id(2)
    gw = hpg * hd
    x = x_ref[0]
    Bm = b_ref[0]
    Cm = c_ref[0]

    lane = lax.broadcasted_iota(jnp.int32, (1, CHUNK), 1)
    pos = ci * CHUNK + lane
    if tile is not None:
        pos = pos & (tile - 1)
    valid = (pos >= lo) & (pos < hi)
    dtT = jnp.where(valid, dtT_ref[...], 0.0)

    ii = lax.broadcasted_iota(jnp.int32, (CHUNK, CHUNK), 0)
    jj = lax.broadcasted_iota(jnp.int32, (CHUNK, CHUNK), 1)
    causal = jj <= ii
    upper = ii <= jj
    if seg < CHUNK:
        sh = int(math.log2(seg))
        same = (ii >> sh) == (jj >> sh)
        causal = causal & same
        upper = upper & same
    Lc = jnp.where(causal, 1.0, 0.0).astype(BF16)
    LT = jnp.where(upper, 1.0, 0.0).astype(BF16)
    eye = jnp.where(ii == jj, 1.0, 0.0).astype(BF16)

    parts = _split3(dtT)
    cum_row = sum(_dot(p, LT) for p in parts) * acol_ref[...]

    def widen(a):
        return jnp.broadcast_to(a[:, None, :], (hpg, hd, CHUNK)).reshape(gw, CHUNK)

    LI = jnp.concatenate([Lc, eye], axis=0)
    G = sum(_dot_nt(LI, widen(p.astype(F32)).astype(BF16)) for p in parts)
    cumcol = G[:CHUNK] * aw_ref[...]
    dtcol = G[CHUNK:]

    cb = _dot_nt(Cm.astype(BF16), Bm.astype(BF16))
    per = LANES // hd
    lanep = lax.broadcasted_iota(jnp.int32, (1, LANES), 1)
    ys = []
    for sp in range(gw // LANES):
        xp = x[:, sp * LANES:(sp + 1) * LANES]
        Ms, Xs = [], []
        for hh in range(per):
            h = sp * per + hh
            ccol = cumcol[:, h * hd:h * hd + 1]
            crow = cum_row[h:h + 1, :]
            dec = jnp.exp(jnp.where(causal, ccol - crow, NEG))
            Ms.append((dec * cb * dtT[h:h + 1, :]).astype(BF16))
            sel = (lanep >= hh * hd) & (lanep < (hh + 1) * hd)
            Xs.append(jnp.where(sel, xp, 0.0).astype(BF16))
        ys.append(_dot(jnp.concatenate(Ms, axis=1), jnp.concatenate(Xs, axis=0)))
    y = jnp.concatenate(ys, axis=1)

    if carry:
        @pl.when(ci == 0)
        def _():
            state[...] = jnp.zeros_like(state)

        ST = state[...]
        yst = _dot(Cm.astype(BF16), ST.astype(BF16))
    else:
        yst = ys_ref[0]
    y = y + yst * jnp.exp(cumcol)

    if carry:
        last = cumcol[CHUNK - 1:CHUNK, :]
        xw = (x * (jnp.exp(last - cumcol) * dtcol)).astype(BF16)
        new = ST * jnp.exp(last) + _dot(Bm.T.astype(BF16), xw)
        state[...] = new
        st_ref[0, 0] = new

    y = y + x * dw_ref[...]
    y = y * _silu(z_ref[0])
    ms = jnp.mean(y * y, axis=-1, keepdims=True)
    y_ref[0] = (y * lax.rsqrt(ms + EPS) * gw_ref[...]).astype(BF16)


def ssd_chunks(z3, act3, dtT, A, D, gate_norm, dims, *, seg, lo, hi, tile, ystate=None, name):
    H, hd, G, N = dims
    hpg = H // G
    gw = hpg * hd
    d_inner = H * hd
    Bt, T, _ = act3.shape
    nchunk = T // CHUNK
    carry = ystate is None
    boff = d_inner // N
    coff = (d_inner + G * N) // N
    in_specs = [
        pl.BlockSpec((1, CHUNK, gw), lambda b, g, c: (b, c, g)),
        pl.BlockSpec((1, CHUNK, gw), lambda b, g, c: (b, c, g)),
        pl.BlockSpec((1, CHUNK, N), lambda b, g, c: (b, c, boff + g)),
        pl.BlockSpec((1, CHUNK, N), lambda b, g, c: (b, c, coff + g)),
        pl.BlockSpec((hpg, CHUNK), lambda b, g, c: (g, b * nchunk + c)),
        pl.BlockSpec((hpg, 1), lambda b, g, c: (g, 0)),
        pl.BlockSpec((1, gw), lambda b, g, c: (0, g)),
        pl.BlockSpec((1, gw), lambda b, g, c: (0, g)),
        pl.BlockSpec((1, gw), lambda b, g, c: (0, g)),
    ]
    args = [z3, act3, act3, act3, dtT, A.reshape(H, 1),
            jnp.repeat(A, hd).reshape(1, d_inner),
            jnp.repeat(D.astype(F32), hd).reshape(1, d_inner),
            gate_norm.astype(F32).reshape(1, d_inner)]
    y_spec = pl.BlockSpec((1, CHUNK, gw), lambda b, g, c: (b, c, g))
    y_shape = jax.ShapeDtypeStruct((Bt, T, d_inner), BF16)
    kern = functools.partial(_ssd_kernel, seg=seg, lo=lo, hi=hi, tile=tile, carry=carry,
                             hpg=hpg, hd=hd, nchunk=nchunk)
    if carry:
        return pl.pallas_call(
            kern, grid=(Bt, G, nchunk), in_specs=in_specs,
            out_specs=[y_spec, pl.BlockSpec((1, 1, N, gw), lambda b, g, c: (b, g, 0, 0))],
            out_shape=[y_shape, jax.ShapeDtypeStruct((Bt, G, N, gw), F32)],
            scratch_shapes=[pltpu.VMEM((N, gw), F32)],
            compiler_params=_cp("parallel", "parallel", "arbitrary"),
            name=name,
        )(*args)
    in_specs.append(pl.BlockSpec((1, CHUNK, gw), lambda b, g, c: (b, c, g)))
    args.append(ystate)
    return pl.pallas_call(
        kern, grid=(Bt, G, nchunk), in_specs=in_specs, out_specs=y_spec, out_shape=y_shape,
        compiler_params=_cp("parallel", "parallel", "arbitrary"),
        name=name,
    )(*args)


def _sstate_kernel(s0_ref, act_ref, dtT_ref, acol_ref, ys_ref, sn_ref, xwT, *, H, hd, G, N, tile, lo):
    d_inner = H * hd
    gw = d_inner // G
    per_blk = CHUNK // tile
    sh = int(math.log2(tile))
    sub = pl.program_id(0) % per_blk
    r0 = pl.multiple_of(sub * tile, tile)
    lane = lax.broadcasted_iota(jnp.int32, (1, CHUNK), 1)
    valid = (lane & (tile - 1)) >= lo
    dtT = jnp.where(valid, dtT_ref[...], 0.0)
    acol = acol_ref[...]

    @pl.when(sub == 0)
    def _():
        ii = lax.broadcasted_iota(jnp.int32, (CHUNK, CHUNK), 0)
        jj = lax.broadcasted_iota(jnp.int32, (CHUNK, CHUNK), 1)
        U = jnp.where((ii > jj) & ((ii >> sh) == (jj >> sh)), 1.0, 0.0).astype(BF16)
        suf = sum(_dot(p, U) for p in _split3(dtT)) * acol
        wd = jnp.exp(suf) * dtT
        wdw = jnp.broadcast_to(wd[:, None, :], (H, hd, CHUNK)).reshape(d_inner, CHUNK)
        xT = act_ref[:, 0:d_inner].T
        xwT[...] = (xT * wdw).astype(BF16)

    own = (lane >> sh) == sub
    last = jnp.sum(jnp.where(own, dtT, 0.0), axis=1, keepdims=True) * acol
    dec = jnp.exp(last)
    dec_col = jnp.broadcast_to(dec[:, None, :], (H, hd, 1)).reshape(d_inner, 1)
    rown = (lax.broadcasted_iota(jnp.int32, (CHUNK, 1), 0) >> sh) == sub
    for g in range(G):
        S0 = s0_ref[0, g * gw:(g + 1) * gw, :]
        Cg = act_ref[pl.ds(r0, tile), d_inner + G * N + g * N:d_inner + G * N + (g + 1) * N]
        ys_ref[pl.ds(r0, tile), g * gw:(g + 1) * gw] = _dot_nt(Cg, S0)
        Bg = jnp.where(rown, act_ref[:, d_inner + g * N:d_inner + (g + 1) * N], 0.0).astype(BF16)
        dS = _dot(xwT[g * gw:(g + 1) * gw, :], Bg)
        sn_ref[0, g * gw:(g + 1) * gw, :] = S0 * dec_col[g * gw:(g + 1) * gw] + dS


def sample_state(s0, act2, dtT, A, dims, *, tile, lo, name):
    H, hd, G, N = dims
    d_inner = H * hd
    nseq = s0.shape[0]
    rows, cdim = act2.shape
    per_blk = CHUNK // tile
    return pl.pallas_call(
        functools.partial(_sstate_kernel, H=H, hd=hd, G=G, N=N, tile=tile, lo=lo),
        grid=(nseq,),
        in_specs=[pl.BlockSpec((1, d_inner, N), lambda b: (b, 0, 0)),
                  pl.BlockSpec((CHUNK, cdim), lambda b: (b // per_blk, 0)),
                  pl.BlockSpec((H, CHUNK), lambda b: (0, b // per_blk)),
                  pl.BlockSpec((H, 1), lambda b: (0, 0))],
        out_specs=[pl.BlockSpec((CHUNK, d_inner), lambda b: (b // per_blk, 0)),
                   pl.BlockSpec((1, d_inner, N), lambda b: (b, 0, 0))],
        out_shape=[jax.ShapeDtypeStruct((rows, d_inner), F32),
                   jax.ShapeDtypeStruct((nseq, d_inner, N), F32)],
        scratch_shapes=[pltpu.VMEM((d_inner, CHUNK), BF16)],
        compiler_params=_cp("arbitrary"),
        name=name,
    )(s0, act2, dtT, A.reshape(H, 1))


def _kvpost_kernel(a_ref, g_ref, cs_ref, rows_ref, kb_ref, *, lora, rope):
    a = a_ref[...]
    c = a[:, :lora]
    ms = jnp.mean(c * c, axis=-1, keepdims=True)
    cn = c * lax.rsqrt(ms + EPS) * g_ref[...]
    u = a[:, lora:lora + LANES] * cs_ref[...]
    kr = u + pltpu.roll(u, rope, 1)
    lane = lax.broadcasted_iota(jnp.int32, (1, LANES), 1)
    rows_ref[:, :lora] = cn
    rows_ref[:, lora:lora + rope] = kr[:, :rope]
    kb_ref[:, :lora] = cn.astype(BF16)
    kb_ref[:, lora:lora + LANES] = jnp.where(lane < rope, kr, 0.0).astype(BF16)


def kv_post(a, g, cs, lora, rope, tm, name):
    M = a.shape[0]
    assert 2 * rope == LANES
    wa = lora + LANES
    return pl.pallas_call(
        functools.partial(_kvpost_kernel, lora=lora, rope=rope),
        grid=(M // tm,),
        in_specs=[pl.BlockSpec((tm, wa), lambda i: (i, 0)),
                  pl.BlockSpec((1, lora), lambda i: (0, 0)),
                  pl.BlockSpec((tm, LANES), lambda i: (i, 0))],
        out_specs=[pl.BlockSpec((tm, lora + rope), lambda i: (i, 0)),
                   pl.BlockSpec((tm, wa), lambda i: (i, 0))],
        out_shape=[jax.ShapeDtypeStruct((M, lora + rope), F32),
                   jax.ShapeDtypeStruct((M, wa), BF16)],
        compiler_params=_cp("parallel"),
        name=name,
    )(a, g.reshape(1, lora), cs)


def _key_inv_rms(kb, waug, heads, nope, qk):
    n = kb.shape[0]
    kn = _dot_nt(waug, kb)
    sq = kn * kn
    ssh = jnp.sum(sq[:heads * nope].reshape(heads, nope, n), axis=1)
    ssr = jnp.sum(sq[heads * nope:], axis=0, keepdims=True)
    return lax.rsqrt((ssh + ssr) / qk + EPS)


def _rms_kernel(kb_ref, w_ref, o_ref, *, heads, nope, qk):
    o_ref[...] = _key_inv_rms(kb_ref[...], w_ref[...], heads, nope, qk)


def key_rms(kb, waug, heads, nope, qk, tk, name):
    M, W = kb.shape
    R = waug.shape[0]
    return pl.pallas_call(
        functools.partial(_rms_kernel, heads=heads, nope=nope, qk=qk),
        grid=(M // tk,),
        in_specs=[pl.BlockSpec((tk, W), lambda i: (i, 0)),
                  pl.BlockSpec((R, W), lambda i: (0, 0))],
        out_specs=pl.BlockSpec((heads, tk), lambda i: (0, i)),
        out_shape=jax.ShapeDtypeStruct((heads, M), F32),
        compiler_params=_cp("parallel"),
        name=name,
    )(kb, waug)


def _past_rms_kernel(pt_ref, *refs, npp, heads, nope, qk, page):
    pages = refs[:npp]
    w_ref, o_ref = refs[npp:]
    for i in range(0, npp, 2):
        kb = jnp.concatenate([pages[i][0], pages[i + 1][0]], axis=0).astype(BF16)
        o_ref[0, :, i * page:(i + 2) * page] = _key_inv_rms(kb, w_ref[...], heads, nope, qk)


def past_key_rms(cache, page_table, waug, heads, nope, qk, npp, name):
    nseq, npages = page_table.shape
    _, page, W = cache.shape
    R = waug.shape[0]
    nsteps = npages // npp
    page_specs = [pl.BlockSpec((1, page, W), (lambda b, s, pt, i=i: (pt[b, s * npp + i], 0, 0)))
                  for i in range(npp)]
    return pl.pallas_call(
        functools.partial(_past_rms_kernel, npp=npp, heads=heads, nope=nope, qk=qk, page=page),
        grid_spec=pltpu.PrefetchScalarGridSpec(
            num_scalar_prefetch=1,
            grid=(nseq, nsteps),
            in_specs=page_specs + [pl.BlockSpec((R, W), lambda b, s, pt: (0, 0))],
            out_specs=pl.BlockSpec((1, heads, npp * page), lambda b, s, pt: (b, 0, s)),
        ),
        out_shape=jax.ShapeDtypeStruct((nseq, heads, npages * page), F32),
        compiler_params=_cp("parallel", "parallel"),
        name=name,
    )(page_table, *([cache] * npp), waug)


def _q_kernel(p_ref, g_ref, wq_ref, cs_ref, qn_ref, kn_ref, wuk_ref, o_ref, *, heads, nope, rope, lora, qk):
    p = p_ref[...]
    ms = jnp.mean(p * p, axis=-1, keepdims=True)
    qa = (p * lax.rsqrt(ms + EPS) * g_ref[...]).astype(BF16)
    lane = lax.broadcasted_iota(jnp.int32, (1, LANES), 1)
    cs = cs_ref[...]
    sc = qn_ref[...] * kn_ref[...] * (1.0 / math.sqrt(qk))
    hw = nope + LANES
    for h in range(heads):
        q = _dot(qa, wq_ref[:, h * hw:(h + 1) * hw])
        qnope = q[:, :nope]
        u = q[:, nope:] * cs
        qr = u + pltpu.roll(u, rope, 1)
        qr = jnp.where(lane < rope, qr, 0.0)
        ss = jnp.sum(qnope * qnope, axis=-1, keepdims=True) + jnp.sum(qr * qr, axis=-1, keepdims=True)
        inv = lax.rsqrt(ss / qk + EPS)
        qn = (qnope * inv * sc[:, :nope]).astype(BF16)
        o_ref[h, :, :lora] = _dot(qn, wuk_ref[h]).astype(BF16)
        o_ref[h, :, lora:lora + LANES] = (qr * inv * sc[:, nope:]).astype(BF16)


def q_side(proj, g, wq_ext, cs, qn_ext, kn_ext, wukT, dims, tm, name):
    heads, nope, rope, lora, qk = dims
    M = proj.shape[0]
    qlora = g.shape[0]
    hw = nope + LANES
    return pl.pallas_call(
        functools.partial(_q_kernel, heads=heads, nope=nope, rope=rope, lora=lora, qk=qk),
        grid=(M // tm,),
        in_specs=[pl.BlockSpec((tm, qlora), lambda i: (i, 0)),
                  pl.BlockSpec((1, qlora), lambda i: (0, 0)),
                  pl.BlockSpec((qlora, heads * hw), lambda i: (0, 0)),
                  pl.BlockSpec((tm, LANES), lambda i: (i, 0)),
                  pl.BlockSpec((1, hw), lambda i: (0, 0)),
                  pl.BlockSpec((1, hw), lambda i: (0, 0)),
                  pl.BlockSpec((heads, nope, lora), lambda i: (0, 0, 0))],
        out_specs=pl.BlockSpec((heads, tm, lora + LANES), lambda i: (0, i, 0)),
        out_shape=jax.ShapeDtypeStruct((heads, M, lora + LANES), BF16),
        compiler_params=_cp("parallel"),
        name=name,
    )(proj, g.reshape(1, qlora), wq_ext, cs, qn_ext, kn_ext, wukT)


def _softmax_update(s, v, m_sc, l_sc, acc_sc):
    m_old = m_sc[...]
    m_new = jnp.maximum(m_old, jnp.max(s, axis=-1, keepdims=True))
    alpha = jnp.exp(m_old - m_new)
    p = jnp.exp(s - m_new)
    l_sc[...] = alpha * l_sc[...] + jnp.sum(p, axis=-1, keepdims=True)
    acc_sc[...] = alpha * acc_sc[...] + _dot(p.astype(BF16), v)
    m_sc[...] = m_new


def _attn_init(m_sc, l_sc, acc_sc):
    m_sc[...] = jnp.full_like(m_sc, NEG)
    l_sc[...] = jnp.zeros_like(l_sc)
    acc_sc[...] = jnp.zeros_like(acc_sc)


def _attn_finish(o_ref, wuv_ref, l_sc, acc_sc, heads, tq, vd):
    inv = 1.0 / l_sc[...]
    for h in range(heads):
        oh = (acc_sc[h * tq:(h + 1) * tq, :] * inv[h * tq:(h + 1) * tq]).astype(BF16)
        o_ref[0, :, h * vd:(h + 1) * vd] = _dot(oh, wuv_ref[h])


def _pattn_kernel(q_ref, k_ref, r_ref, wuv_ref, o_ref, m_sc, l_sc, acc_sc, *, heads, tq, ck, lora, vd):
    qi = pl.program_id(1)
    W = q_ref.shape[-1]
    Q = q_ref[:, 0].reshape(heads * tq, W)
    _attn_init(m_sc, l_sc, acc_sc)

    def chunk(c, masked):
        k0 = pl.multiple_of(c * ck, ck)
        Kc = k_ref[0, pl.ds(k0, ck), :]
        s = _dot_nt(Q, Kc).reshape(heads, tq, ck) * r_ref[0, c][:, None, :]
        if masked:
            ti = qi * tq + lax.broadcasted_iota(jnp.int32, (1, tq, ck), 1)
            kj = k0 + lax.broadcasted_iota(jnp.int32, (1, tq, ck), 2)
            s = jnp.where(kj <= ti, s, NEG)
        _softmax_update(s.reshape(heads * tq, ck), Kc[:, :lora], m_sc, l_sc, acc_sc)

    nfull = (qi * tq) // ck

    def body(c, carry):
        chunk(c, False)
        return carry

    lax.fori_loop(0, nfull, body, 0)
    chunk(nfull, True)
    _attn_finish(o_ref, wuv_ref, l_sc, acc_sc, heads, tq, vd)


def prompt_attention(q4, kb3, r4, wuv, tq, ck, name):
    heads, B, T, W = q4.shape
    Tk = kb3.shape[1]
    lora, vd = wuv.shape[1], wuv.shape[2]
    return pl.pallas_call(
        functools.partial(_pattn_kernel, heads=heads, tq=tq, ck=ck, lora=lora, vd=vd),
        grid=(B, T // tq),
        in_specs=[pl.BlockSpec((heads, 1, tq, W), lambda b, i: (0, b, i, 0)),
                  pl.BlockSpec((1, Tk, W), lambda b, i: (b, 0, 0)),
                  pl.BlockSpec((1, Tk // ck, heads, ck), lambda b, i: (b, 0, 0, 0)),
                  pl.BlockSpec((heads, lora, vd), lambda b, i: (0, 0, 0))],
        out_specs=pl.BlockSpec((1, tq, heads * vd), lambda b, i: (b, i, 0)),
        out_shape=jax.ShapeDtypeStruct((B, T, heads * vd), F32),
        scratch_shapes=[pltpu.VMEM((heads * tq, 1), F32), pltpu.VMEM((heads * tq, 1), F32),
                        pltpu.VMEM((heads * tq, lora), F32)],
        compiler_params=_cp("parallel", "arbitrary"),
        name=name,
    )(q4, kb3, r4, wuv)


def _sattn_kernel(pt_ref, q_ref, *refs, npp, nsteps, heads, tq, page, lora, vd):
    pages = refs[:npp]
    rp_ref, kn_ref, rn_ref, wuv_ref, o_ref, m_sc, l_sc, acc_sc = refs[npp:]
    step = pl.program_id(1)

    @pl.when(step == 0)
    def _():
        _attn_init(m_sc, l_sc, acc_sc)

    Q = q_ref[0]
    for i in range(npp):
        Kp = pages[i][0].astype(BF16)
        s = _dot_nt(Q, Kp).reshape(heads, tq, page) * rp_ref[0, :, i * page:(i + 1) * page][:, None, :]
        _softmax_update(s.reshape(heads * tq, page), Kp[:, :lora], m_sc, l_sc, acc_sc)

    @pl.when(step == nsteps - 1)
    def _():
        Kn = kn_ref[0]
        s = _dot_nt(Q, Kn).reshape(heads, tq, page) * rn_ref[0][:, None, :]
        ti = lax.broadcasted_iota(jnp.int32, (1, tq, page), 1)
        kj = lax.broadcasted_iota(jnp.int32, (1, tq, page), 2)
        s = jnp.where(kj <= ti, s, NEG)
        _softmax_update(s.reshape(heads * tq, page), Kn[:, :lora], m_sc, l_sc, acc_sc)
        _attn_finish(o_ref, wuv_ref, l_sc, acc_sc, heads, tq, vd)


def sample_attention(page_table, q3, cache, r_past, knew, rnew, wuv, tq, npp, name):
    nseq, npages = page_table.shape
    _, page, W = cache.shape
    heads, lora, vd = wuv.shape
    nsteps = npages // npp
    page_specs = [pl.BlockSpec((1, page, W), (lambda b, s, pt, i=i: (pt[b, s * npp + i], 0, 0)))
                  for i in range(npp)]
    return pl.pallas_call(
        functools.partial(_sattn_kernel, npp=npp, nsteps=nsteps, heads=heads, tq=tq, page=page,
                          lora=lora, vd=vd),
        grid_spec=pltpu.PrefetchScalarGridSpec(
            num_scalar_prefetch=1,
            grid=(nseq, nsteps),
            in_specs=[pl.BlockSpec((1, heads * tq, W), lambda b, s, pt: (b, 0, 0))] + page_specs + [
                pl.BlockSpec((1, heads, npp * page), lambda b, s, pt: (b, 0, s)),
                pl.BlockSpec((1, page, W), lambda b, s, pt: (b, 0, 0)),
                pl.BlockSpec((1, heads, page), lambda b, s, pt: (b, 0, 0)),
                pl.BlockSpec((heads, lora, vd), lambda b, s, pt: (0, 0, 0))],
            out_specs=pl.BlockSpec((1, tq, heads * vd), lambda b, s, pt: (b, 0, 0)),
            scratch_shapes=[pltpu.VMEM((heads * tq, 1), F32), pltpu.VMEM((heads * tq, 1), F32),
                            pltpu.VMEM((heads * tq, lora), F32)],
        ),
        out_shape=jax.ShapeDtypeStruct((nseq, tq, heads * vd), F32),
        compiler_params=_cp("parallel", "arbitrary"),
        name=name,
    )(page_table, q3, *([cache] * npp), r_past, knew, rnew, wuv)


def _rope_table(pos, rope):
    inv = 1.0 / (ROPE_BASE ** (jnp.arange(0, rope, 2, dtype=F32) / rope))
    f = pos.astype(F32)[:, None] * inv[None, :]
    emb = jnp.concatenate([f, f], -1)
    return jnp.concatenate([jnp.cos(emb), jnp.sin(emb)], -1)


def _rot_cols(w, rope):
    h = rope // 2
    return jnp.concatenate([-w[..., h:], w[..., :h]], -1)


def kernel(x_prompt, x_sample, state_ssm, state_conv, cache_kv, page_table, meta_tokens,
           a_norm, a_w_in, a_conv_w, a_conv_b, a_dt_bias, a_A_log, a_D, a_gate_norm, a_w_out,
           kv_norm, w_kv_a, kv_a_norm, w_uk, w_uv, k_norm,
           b_norm, b_w_in, b_q_a_norm, b_w_q, b_q_norm, b_w_out):
    bt, L, dm = x_prompt.shape
    nseq, S, _ = x_sample.shape
    n_meta = meta_tokens.shape[0]
    n_a = a_w_in.shape[0]
    n_b = b_w_in.shape[0]
    _, _, H, hd, N = state_ssm.shape
    cw, cdim = a_conv_w.shape[1:]
    d_inner = a_w_out.shape[1]
    G = (cdim - d_inner) // (2 * N)
    sdims = (H, hd, G, N)
    lora, heads, nope = w_uk.shape
    vd = w_uv.shape[2]
    kvrow = cache_kv.shape[2]
    rope = kvrow - lora
    qk = nope + rope
    qlora = b_q_a_norm.shape[1]
    page = cache_kv.shape[1]
    past = page_table.shape[1] * page
    tile = SUBLANES
    assert cw - 1 + S + 1 == tile and CHUNK % tile == 0 and (nseq * tile) % CHUNK == 0

    T = L + n_meta
    Tp = -(-T // CHUNK) * CHUNK
    Mp = bt * Tp
    Ms = nseq * S
    tmp = _pick(Mp, 512)
    tms = _pick(Ms, 512)

    xp = jnp.concatenate([jnp.broadcast_to(meta_tokens[None], (bt, n_meta, dm)), x_prompt,
                          jnp.zeros((bt, Tp - T, dm), F32)], 1).reshape(Mp, dm)
    xs = x_sample.reshape(Ms, dm)

    ssm_p, conv_p, ssm_s, conv_s = [], [], [], []
    for i in range(n_a):
        w_main = a_w_in[i][:, :d_inner + cdim].astype(BF16)
        w_dt = a_w_in[i][:, d_inner + cdim:]
        w_out = a_w_out[i].astype(BF16)
        A = -jnp.exp(a_A_log[i].astype(F32))
        tn = _pick(d_inner + cdim, 1024)

        proj = norm_matmul(xp, a_norm[i], w_main, tmp, tn, name=f"a{i}_in_p")
        dt = dt_proj(xp, a_norm[i], w_dt, a_dt_bias[i], tmp, name=f"a{i}_dt_p")
        proj3 = proj.reshape(bt, Tp, d_inner + cdim)
        act = conv_silu(proj3, d_inner, a_conv_w[i], a_conv_b[i], 256,True, name=f"a{i}_conv_p")
        y, st = ssd_chunks(proj3, act, dt[:, :H].T, A, a_D[i], a_gate_norm[i], sdims,
                           seg=CHUNK, lo=0, hi=T, tile=None, name=f"a{i}_ssd_p")
        xp = matmul_res(y.reshape(Mp, d_inner), w_out, xp, tmp, dm, 512, name=f"a{i}_out_p")
        conv_p.append(proj3[:, T - (cw - 1):T, d_inner:])
        ssm_p.append(st.reshape(bt, G, N, H // G, hd).transpose(0, 1, 3, 4, 2).reshape(bt, H, hd, N))

        proj_s = norm_matmul(xs, a_norm[i], w_main, tms, tn, name=f"a{i}_in_s")
        dt_s = dt_proj(xs, a_norm[i], w_dt, a_dt_bias[i], tms, name=f"a{i}_dt_s")
        ps3 = proj_s.reshape(nseq, S, d_inner + cdim)
        z8 = jnp.concatenate([jnp.zeros((nseq, tile - S, d_inner), F32), ps3[..., :d_inner]], 1)
        full8 = jnp.concatenate([jnp.zeros((nseq, 1, cdim), F32), state_conv[i], ps3[..., d_inner:]], 1)
        conv_s.append(full8[:, tile - (cw - 1):])
        dt8T = jnp.concatenate([jnp.zeros((nseq, tile - S, H), F32), dt_s[:, :H].reshape(nseq, S, H)],
                               1).reshape(nseq * tile, H).T
        act8 = conv_silu(full8.reshape(1, nseq * tile, cdim), 0, a_conv_w[i], a_conv_b[i], 256,False,
                         name=f"a{i}_conv_s")
        ys, s_new = sample_state(state_ssm[i].reshape(nseq, d_inner, N), act8[0], dt8T, A, sdims,
                                 tile=tile, lo=tile - S, name=f"a{i}_state_s")
        nblk = nseq * tile // CHUNK
        y8 = ssd_chunks(z8.reshape(nblk, CHUNK, d_inner), act8.reshape(nblk, CHUNK, cdim), dt8T, A, a_D[i],
                        a_gate_norm[i], sdims, seg=tile, lo=tile - S, hi=tile, tile=tile,
                        ystate=ys.reshape(nblk, CHUNK, d_inner), name=f"a{i}_ssd_s")
        y_s = y8.reshape(nseq, tile, d_inner)[:, tile - S:].reshape(Ms, d_inner)
        xs = matmul_res(y_s, w_out, xs, tms, dm, 512, name=f"a{i}_out_s")
        ssm_s.append(s_new.reshape(nseq, H, hd, N))

    wa = lora + LANES
    w_kv_ext = jnp.concatenate([w_kv_a, _rot_cols(w_kv_a[:, lora:], rope)], 1).astype(BF16)
    cs_p = jnp.tile(_rope_table(jnp.arange(Tp), rope), (bt, 1))
    cs_s = jnp.tile(_rope_table(past + jnp.arange(S), rope), (nseq, 1))
    wukT2 = w_uk.transpose(1, 2, 0).reshape(heads * nope, lora)
    waug = jnp.zeros((heads * nope + rope, wa), F32)
    waug = waug.at[:heads * nope, :lora].set(wukT2)
    waug = waug.at[heads * nope:, lora:lora + rope].set(jnp.eye(rope, dtype=F32)).astype(BF16)
    wukT = w_uk.transpose(1, 2, 0).astype(BF16)
    wuv = w_uv.transpose(1, 0, 2).astype(BF16)

    a_p = norm_matmul(xp, kv_norm, w_kv_ext, tmp, wa, name="kv_a_p")
    rows_p, kb_p = kv_post(a_p, kv_a_norm, cs_p, lora, rope, tmp, name="kv_post_p")
    rT_p = key_rms(kb_p, waug, heads, nope, qk, _pick(Mp, 256), name="kv_rms_p")
    a_s = norm_matmul(xs, kv_norm, w_kv_ext, tms, wa, name="kv_a_s")
    rows_s, kb_s = kv_post(a_s, kv_a_norm, cs_s, lora, rope, tms, name="kv_post_s")
    rT_s = key_rms(kb_s, waug, heads, nope, qk, _pick(Ms, 256), name="kv_rms_s")
    npp = min(16, page_table.shape[1])
    r_past = past_key_rms(cache_kv, page_table, waug[:, :kvrow], heads, nope, qk, npp, name="kv_rms_past")

    ck = 2 * CHUNK
    Tk = -(-Tp // ck) * ck
    kb3 = jnp.pad(kb_p.reshape(bt, Tp, wa), ((0, 0), (0, Tk - Tp), (0, 0)))
    r4 = jnp.pad(rT_p.reshape(heads, bt, Tp), ((0, 0), (0, 0), (0, Tk - Tp)), constant_values=1.0)
    r4 = r4.reshape(heads, bt, Tk // ck, ck).transpose(1, 2, 0, 3)
    knew = jnp.pad(kb_s[:, :kvrow].reshape(nseq, S, kvrow), ((0, 0), (0, page - S), (0, 0)))
    rnew = jnp.pad(rT_s.reshape(heads, nseq, S).transpose(1, 0, 2), ((0, 0), (0, 0), (0, page - S)),
                   constant_values=1.0)

    def ext(v):
        return jnp.pad(v.astype(F32), (0, LANES - rope)).reshape(1, nope + LANES)

    qdims = (heads, nope, rope, lora, qk)
    tq = CHUNK
    for j in range(n_b):
        w_in = b_w_in[j].astype(BF16)
        wq3 = b_w_q[j].reshape(qlora, heads, qk)
        wq_ext = jnp.concatenate([wq3, _rot_cols(wq3[..., nope:], rope)], -1).reshape(qlora, heads * (nope + LANES))
        wq_ext = wq_ext.astype(BF16)
        w_out = b_w_out[j].astype(BF16)
        tn = _pick(w_in.shape[1], 1280)

        proj = norm_matmul(xp, b_norm[j], w_in, tmp, tn, name=f"b{j}_in_p")
        q = q_side(proj, b_q_a_norm[j], wq_ext, cs_p, ext(b_q_norm[j]), ext(k_norm), wukT, qdims,
                   _pick(Mp, 256), name=f"b{j}_q_p")
        o = prompt_attention(q.reshape(heads, bt, Tp, wa), kb3, r4, wuv, tq, ck, name=f"b{j}_attn_p")
        xp = matmul_res(o.reshape(Mp, heads * vd), w_out, xp, tmp, dm, 512, name=f"b{j}_out_p",
                        gate=proj, gate_col0=qlora)

        proj_s = norm_matmul(xs, b_norm[j], w_in, tms, tn, name=f"b{j}_in_s")
        q_s = q_side(proj_s, b_q_a_norm[j], wq_ext, cs_s, ext(b_q_norm[j]), ext(k_norm), wukT, qdims,
                     _pick(Ms, 256), name=f"b{j}_q_s")
        q8 = jnp.pad(q_s[..., :kvrow].reshape(heads, nseq, S, kvrow), ((0, 0), (0, 0), (0, tile - S), (0, 0)))
        q8 = q8.transpose(1, 0, 2, 3).reshape(nseq, heads * tile, kvrow)
        o_s = sample_attention(page_table, q8, cache_kv, r_past, knew, rnew, wuv, tile, npp,
                               name=f"b{j}_attn_s")
        xs = matmul_res(o_s[:, :S].reshape(Ms, heads * vd), w_out, xs, tms, dm, 512, name=f"b{j}_out_s",
                        gate=proj_s, gate_col0=qlora)

    y_prompt = xp.reshape(bt, Tp, dm)[:, n_meta:T]
    kv_p = rows_p.reshape(bt, Tp, kvrow)[:, :T]
    return (y_prompt, xs.reshape(nseq, S, dm), jnp.stack(ssm_p), jnp.stack(conv_p), kv_p,
            jnp.stack(ssm_s), jnp.stack(conv_s), rows_s.reshape(nseq, S, kvrow))
```

```python
import functools
import math

import jax
import jax.numpy as jnp
from jax import lax
from jax.experimental import pallas as pl
from jax.experimental.pallas import tpu as pltpu

F32 = jnp.float32
BF16 = jnp.bfloat16
EPS = 1e-6
ROPE_BASE = 10000.0
LANES = 128
SUBLANES = 8
CHUNK = 128
NEG = -1e30
VMEM_LIMIT = 48 * 1024 * 1024


def _cp(*sem):
    return pltpu.CompilerParams(dimension_semantics=sem, vmem_limit_bytes=VMEM_LIMIT)


def _dot(a, b):
    return jnp.dot(a, b, preferred_element_type=F32)


def _dot_nt(a, b):
    return lax.dot_general(a, b, (((1,), (1,)), ((), ())), preferred_element_type=F32)


def _split3(a):
    h = a.astype(BF16)
    r = a - h.astype(F32)
    m = r.astype(BF16)
    l = (r - m.astype(F32)).astype(BF16)
    return h, m, l


def _silu(v):
    return v * jax.nn.sigmoid(v)


def _pick(n, target):
    best = None
    for t in range(8, min(n, target) + 1, 8):
        if n % t == 0:
            best = t
    assert best is not None, (n, target)
    return best


def _norm_matmul_kernel(x_ref, g_ref, w_ref, o_ref, xn_ref):
    @pl.when(pl.program_id(1) == 0)
    def _():
        x = x_ref[...]
        ms = jnp.mean(x * x, axis=-1, keepdims=True)
        xn_ref[...] = (x * lax.rsqrt(ms + EPS) * g_ref[...]).astype(BF16)

    o_ref[...] = _dot(xn_ref[...], w_ref[...])


def norm_matmul(x, g, w, tm, tn, name):
    M, K = x.shape
    N = w.shape[1]
    return pl.pallas_call(
        _norm_matmul_kernel,
        grid=(M // tm, N // tn),
        in_specs=[pl.BlockSpec((tm, K), lambda i, j: (i, 0)),
                  pl.BlockSpec((1, K), lambda i, j: (0, 0)),
                  pl.BlockSpec((K, tn), lambda i, j: (0, j))],
        out_specs=pl.BlockSpec((tm, tn), lambda i, j: (i, j)),
        out_shape=jax.ShapeDtypeStruct((M, N), F32),
        scratch_shapes=[pltpu.VMEM((tm, K), BF16)],
        compiler_params=_cp("parallel", "arbitrary"),
        name=name,
    )(x, g.reshape(1, K), w)


def _dt_kernel(x_ref, g_ref, wh_ref, wl_ref, b_ref, o_ref):
    x = x_ref[...]
    ms = jnp.mean(x * x, axis=-1, keepdims=True)
    xn = x * lax.rsqrt(ms + EPS) * g_ref[...]
    xh = xn.astype(BF16)
    xl = (xn - xh.astype(F32)).astype(BF16)
    v = _dot(xh, wh_ref[...]) + _dot(xl, wh_ref[...]) + _dot(xh, wl_ref[...]) + b_ref[...]
    o_ref[...] = jnp.maximum(v, 0.0) + jnp.log(1.0 + jnp.exp(-jnp.abs(v)))


def dt_proj(x, g, w_dt, bias, tm, name):
    M, K = x.shape
    H = w_dt.shape[1]
    Hp = -(-H // LANES) * LANES
    wp = jnp.pad(w_dt, ((0, 0), (0, Hp - H)))
    wh = wp.astype(BF16)
    wl = (wp - wh.astype(F32)).astype(BF16)
    bp = jnp.pad(bias.astype(F32), (0, Hp - H)).reshape(1, Hp)
    return pl.pallas_call(
        _dt_kernel,
        grid=(M // tm,),
        in_specs=[pl.BlockSpec((tm, K), lambda i: (i, 0)),
                  pl.BlockSpec((1, K), lambda i: (0, 0)),
                  pl.BlockSpec((K, Hp), lambda i: (0, 0)),
                  pl.BlockSpec((K, Hp), lambda i: (0, 0)),
                  pl.BlockSpec((1, Hp), lambda i: (0, 0))],
        out_specs=pl.BlockSpec((tm, Hp), lambda i: (i, 0)),
        out_shape=jax.ShapeDtypeStruct((M, Hp), F32),
        compiler_params=_cp("parallel"),
        name=name,
    )(x, g.reshape(1, K), wh, wl, bp)


def _mmres_kernel(*refs, gated, nk):
    if gated:
        a_ref, gate_ref, w_ref, r_ref, o_ref, acc = refs
    else:
        a_ref, w_ref, r_ref, o_ref, acc = refs
    k = pl.program_id(2)

    @pl.when(k == 0)
    def _():
        acc[...] = jnp.zeros_like(acc)

    a = a_ref[...]
    if gated:
        a = (a * _silu(gate_ref[...])).astype(BF16)
    acc[...] += _dot(a, w_ref[...])

    @pl.when(k == nk - 1)
    def _():
        o_ref[...] = r_ref[...] + acc[...]


def matmul_res(a, w, res, tm, tn, tk, name, gate=None, gate_col0=0):
    M, K = a.shape
    N = w.shape[1]
    nk = K // tk
    gated = gate is not None
    in_specs = [pl.BlockSpec((tm, tk), lambda i, j, k: (i, k))]
    args = [a]
    if gated:
        goff = gate_col0 // tk
        assert goff * tk == gate_col0
        in_specs.append(pl.BlockSpec((tm, tk), lambda i, j, k: (i, goff + k)))
        args.append(gate)
    in_specs += [pl.BlockSpec((tk, tn), lambda i, j, k: (k, j)),
                 pl.BlockSpec((tm, tn), lambda i, j, k: (i, j))]
    args += [w, res]
    return pl.pallas_call(
        functools.partial(_mmres_kernel, gated=gated, nk=nk),
        grid=(M // tm, N // tn, nk),
        in_specs=in_specs,
        out_specs=pl.BlockSpec((tm, tn), lambda i, j, k: (i, j)),
        out_shape=jax.ShapeDtypeStruct((M, N), F32),
        scratch_shapes=[pltpu.VMEM((tm, tn), F32)],
        compiler_params=_cp("parallel", "parallel", "arbitrary"),
        name=name,
    )(*args)


def _conv_kernel(x_ref, w_ref, b_ref, o_ref, *, width, zero_head):
    x = x_ref[0]
    rid = lax.broadcasted_iota(jnp.int32, x.shape, 0)
    acc = b_ref[...] + x * w_ref[width - 1:width, :]
    for k in range(width - 1):
        s = width - 1 - k
        xs = pltpu.roll(x, s, 0)
        if zero_head:
            xs = jnp.where(rid < s, 0.0, xs)
        acc = acc + xs * w_ref[k:k + 1, :]
    o_ref[0] = _silu(acc)


def conv_silu(inp, col0, w, b, tc, zero_head, name):
    Bt, T, _ = inp.shape
    W, C = w.shape
    off = col0 // tc
    assert off * tc == col0 and C % tc == 0
    return pl.pallas_call(
        functools.partial(_conv_kernel, width=W, zero_head=zero_head),
        grid=(Bt, C // tc),
        in_specs=[pl.BlockSpec((1, T, tc), lambda b, j: (b, 0, off + j)),
                  pl.BlockSpec((W, tc), lambda b, j: (0, j)),
                  pl.BlockSpec((1, tc), lambda b, j: (0, j))],
        out_specs=pl.BlockSpec((1, T, tc), lambda b, j: (b, 0, j)),
        out_shape=jax.ShapeDtypeStruct((Bt, T, C), F32),
        compiler_params=_cp("parallel", "parallel"),
        name=name,
    )(inp, w, b.reshape(1, C))


def _ssd_kernel(*refs, seg, lo, hi, tile, carry, hpg, hd, nchunk):
    if carry:
        (z_ref, x_ref, b_ref, c_ref, dtT_ref, acol_ref, aw_ref, dw_ref, gw_ref,
         y_ref, st_ref, state) = refs
    else:
        (z_ref, x_ref, b_ref, c_ref, dtT_ref, acol_ref, aw_ref, dw_ref, gw_ref,
         ys_ref, y_ref) = refs
    ci = pl.program_id(2)
    gw = hpg * hd
    x = x_ref[0]
    Bm = b_ref[0]
    Cm = c_ref[0]

    lane = lax.broadcasted_iota(jnp.int32, (1, CHUNK), 1)
    pos = ci * CHUNK + lane
    if tile is not None:
        pos = pos & (tile - 1)
    valid = (pos >= lo) & (pos < hi)
    dtT = jnp.where(valid, dtT_ref[...], 0.0)

    ii = lax.broadcasted_iota(jnp.int32, (CHUNK, CHUNK), 0)
    jj = lax.broadcasted_iota(jnp.int32, (CHUNK, CHUNK), 1)
    causal = jj <= ii
    upper = ii <= jj
    if seg < CHUNK:
        sh = int(math.log2(seg))
        same = (ii >> sh) == (jj >> sh)
        causal = causal & same
        upper = upper & same
    Lc = jnp.where(causal, 1.0, 0.0).astype(BF16)
    LT = jnp.where(upper, 1.0, 0.0).astype(BF16)
    eye = jnp.where(ii == jj, 1.0, 0.0).astype(BF16)

    parts = _split3(dtT)
    cum_row = sum(_dot(p, LT) for p in parts) * acol_ref[...]

    def widen(a):
        return jnp.broadcast_to(a[:, None, :], (hpg, hd, CHUNK)).reshape(gw, CHUNK)

    LI = jnp.concatenate([Lc, eye], axis=0)
    G = sum(_dot_nt(LI, widen(p.astype(F32)).astype(BF16)) for p in parts)
    cumcol = G[:CHUNK] * aw_ref[...]
    dtcol = G[CHUNK:]

    cb = _dot_nt(Cm.astype(BF16), Bm.astype(BF16))
    per = LANES // hd
    lanep = lax.broadcasted_iota(jnp.int32, (1, LANES), 1)
    ys = []
    for sp in range(gw // LANES):
        xp = x[:, sp * LANES:(sp + 1) * LANES]
        Ms, Xs = [], []
        for hh in range(per):
            h = sp * per + hh
            ccol = cumcol[:, h * hd:h * hd + 1]
            crow = cum_row[h:h + 1, :]
            dec = jnp.exp(jnp.where(causal, ccol - crow, NEG))
            Ms.append((dec * cb * dtT[h:h + 1, :]).astype(BF16))
            sel = (lanep >= hh * hd) & (lanep < (hh + 1) * hd)
            Xs.append(jnp.where(sel, xp, 0.0).astype(BF16))
        ys.append(_dot(jnp.concatenate(Ms, axis=1), jnp.concatenate(Xs, axis=0)))
    y = jnp.concatenate(ys, axis=1)

    if carry:
        @pl.when(ci == 0)
        def _():
            state[...] = jnp.zeros_like(state)

        ST = state[...]
        yst = _dot(Cm.astype(BF16), ST.astype(BF16))
    else:
        yst = ys_ref[0]
    y = y + yst * jnp.exp(cumcol)

    if carry:
        last = cumcol[CHUNK - 1:CHUNK, :]
        xw = (x * (jnp.exp(last - cumcol) * dtcol)).astype(BF16)
        new = ST * jnp.exp(last) + _dot(Bm.T.astype(BF16), xw)
        state[...] = new
        st_ref[0, 0] = new

    y = y + x * dw_ref[...]
    y = y * _silu(z_ref[0])
    ms = jnp.mean(y * y, axis=-1, keepdims=True)
    y_ref[0] = (y * lax.rsqrt(ms + EPS) * gw_ref[...]).astype(BF16)


def ssd_chunks(z3, act3, dtT, A, D, gate_norm, dims, *, seg, lo, hi, tile, ystate=None, name):
    H, hd, G, N = dims
    hpg = H // G
    gw = hpg * hd
    d_inner = H * hd
    Bt, T, _ = act3.shape
    nchunk = T // CHUNK
    carry = ystate is None
    boff = d_inner // N
    coff = (d_inner + G * N) // N
    in_specs = [
        pl.BlockSpec((1, CHUNK, gw), lambda b, g, c: (b, c, g)),
        pl.BlockSpec((1, CHUNK, gw), lambda b, g, c: (b, c, g)),
        pl.BlockSpec((1, CHUNK, N), lambda b, g, c: (b, c, boff + g)),
        pl.BlockSpec((1, CHUNK, N), lambda b, g, c: (b, c, coff + g)),
        pl.BlockSpec((hpg, CHUNK), lambda b, g, c: (g, b * nchunk + c)),
        pl.BlockSpec((hpg, 1), lambda b, g, c: (g, 0)),
        pl.BlockSpec((1, gw), lambda b, g, c: (0, g)),
        pl.BlockSpec((1, gw), lambda b, g, c: (0, g)),
        pl.BlockSpec((1, gw), lambda b, g, c: (0, g)),
    ]
    args = [z3, act3, act3, act3, dtT, A.reshape(H, 1),
            jnp.repeat(A, hd).reshape(1, d_inner),
            jnp.repeat(D.astype(F32), hd).reshape(1, d_inner),
            gate_norm.astype(F32).reshape(1, d_inner)]
    y_spec = pl.BlockSpec((1, CHUNK, gw), lambda b, g, c: (b, c, g))
    y_shape = jax.ShapeDtypeStruct((Bt, T, d_inner), BF16)
    kern = functools.partial(_ssd_kernel, seg=seg, lo=lo, hi=hi, tile=tile, carry=carry,
                             hpg=hpg, hd=hd, nchunk=nchunk)
    if carry:
        return pl.pallas_call(
            kern, grid=(Bt, G, nchunk), in_specs=in_specs,
            out_specs=[y_spec, pl.BlockSpec((1, 1, N, gw), lambda b, g, c: (b, g, 0, 0))],
            out_shape=[y_shape, jax.ShapeDtypeStruct((Bt, G, N, gw), F32)],
            scratch_shapes=[pltpu.VMEM((N, gw), F32)],
            compiler_params=_cp("parallel", "parallel", "arbitrary"),
            name=name,
        )(*args)
    in_specs.append(pl.BlockSpec((1, CHUNK, gw), lambda b, g, c: (b, c, g)))
    args.append(ystate)
    return pl.pallas_call(
        kern, grid=(Bt, G, nchunk), in_specs=in_specs, out_specs=y_spec, out_shape=y_shape,
        compiler_params=_cp("parallel", "parallel", "arbitrary"),
        name=name,
    )(*args)


def _sstate_kernel(*refs, H, hd, G, N, tile, lo, chained):
    if chained:
        s0_ref, act_ref, dtT_ref, acol_ref, _, ys_ref, sn_ref, xwT = refs
    else:
        s0_ref, act_ref, dtT_ref, acol_ref, ys_ref, sn_ref, xwT = refs
    d_inner = H * hd
    gw = d_inner // G
    per_blk = CHUNK // tile
    sh = int(math.log2(tile))
    sub = pl.program_id(0) % per_blk
    r0 = pl.multiple_of(sub * tile, tile)
    lane = lax.broadcasted_iota(jnp.int32, (1, CHUNK), 1)
    valid = (lane & (tile - 1)) >= lo
    dtT = jnp.where(valid, dtT_ref[...], 0.0)
    acol = acol_ref[...]

    @pl.when(sub == 0)
    def _():
        ii = lax.broadcasted_iota(jnp.int32, (CHUNK, CHUNK), 0)
        jj = lax.broadcasted_iota(jnp.int32, (CHUNK, CHUNK), 1)
        U = jnp.where((ii > jj) & ((ii >> sh) == (jj >> sh)), 1.0, 0.0).astype(BF16)
        suf = sum(_dot(p, U) for p in _split3(dtT)) * acol
        wd = jnp.exp(suf) * dtT
        wdw = jnp.broadcast_to(wd[:, None, :], (H, hd, CHUNK)).reshape(d_inner, CHUNK)
        xT = act_ref[:, 0:d_inner].T
        xwT[...] = (xT * wdw).astype(BF16)

    own = (lane >> sh) == sub
    last = jnp.sum(jnp.where(own, dtT, 0.0), axis=1, keepdims=True) * acol
    dec = jnp.exp(last)
    dec_col = jnp.broadcast_to(dec[:, None, :], (H, hd, 1)).reshape(d_inner, 1)
    rown = (lax.broadcasted_iota(jnp.int32, (CHUNK, 1), 0) >> sh) == sub
    for g in range(G):
        S0 = s0_ref[0, 0, g * gw:(g + 1) * gw, :]
        Cg = act_ref[pl.ds(r0, tile), d_inner + G * N + g * N:d_inner + G * N + (g + 1) * N]
        ys_ref[pl.ds(r0, tile), g * gw:(g + 1) * gw] = _dot_nt(Cg, S0)
        Bg = jnp.where(rown, act_ref[:, d_inner + g * N:d_inner + (g + 1) * N], 0.0).astype(BF16)
        dS = _dot(xwT[g * gw:(g + 1) * gw, :], Bg)
        sn_ref[0, 0, g * gw:(g + 1) * gw, :] = S0 * dec_col[g * gw:(g + 1) * gw] + dS


def sample_state(s_all, layer, act2, dtT, A, dims, *, tile, lo, prev, name):
    H, hd, G, N = dims
    d_inner = H * hd
    nseq = s_all.shape[1]
    rows, cdim = act2.shape
    per_blk = CHUNK // tile
    chained = prev is not None
    in_specs = [pl.BlockSpec((1, 1, d_inner, N), lambda b: (layer, b, 0, 0)),
                pl.BlockSpec((CHUNK, cdim), lambda b: (b // per_blk, 0)),
                pl.BlockSpec((H, CHUNK), lambda b: (0, b // per_blk)),
                pl.BlockSpec((H, 1), lambda b: (0, 0))]
    args = [s_all, act2, dtT, A.reshape(H, 1)]
    if chained:
        in_specs.append(pl.BlockSpec(memory_space=pl.ANY))
        args.append(prev)
    return pl.pallas_call(
        functools.partial(_sstate_kernel, H=H, hd=hd, G=G, N=N, tile=tile, lo=lo, chained=chained),
        grid=(nseq,),
        in_specs=in_specs,
        out_specs=[pl.BlockSpec((CHUNK, d_inner), lambda b: (b // per_blk, 0)),
                   pl.BlockSpec((1, 1, d_inner, N), lambda b: (layer, b, 0, 0))],
        out_shape=[jax.ShapeDtypeStruct((rows, d_inner), F32),
                   jax.ShapeDtypeStruct(s_all.shape, F32)],
        scratch_shapes=[pltpu.VMEM((d_inner, CHUNK), BF16)],
        input_output_aliases={4: 1} if chained else {},
        compiler_params=_cp("arbitrary"),
        name=name,
    )(*args)


def _kvpost_kernel(a_ref, g_ref, cs_ref, rows_ref, kb_ref, *, lora, rope):
    a = a_ref[...]
    c = a[:, :lora]
    ms = jnp.mean(c * c, axis=-1, keepdims=True)
    cn = c * lax.rsqrt(ms + EPS) * g_ref[...]
    u = a[:, lora:lora + LANES] * cs_ref[...]
    kr = u + pltpu.roll(u, rope, 1)
    lane = lax.broadcasted_iota(jnp.int32, (1, LANES), 1)
    rows_ref[:, :lora] = cn
    rows_ref[:, lora:lora + rope] = kr[:, :rope]
    kb_ref[:, :lora] = cn.astype(BF16)
    kb_ref[:, lora:lora + LANES] = jnp.where(lane < rope, kr, 0.0).astype(BF16)


def kv_post(a, g, cs, lora, rope, tm, name):
    M = a.shape[0]
    assert 2 * rope == LANES
    wa = lora + LANES
    return pl.pallas_call(
        functools.partial(_kvpost_kernel, lora=lora, rope=rope),
        grid=(M // tm,),
        in_specs=[pl.BlockSpec((tm, wa), lambda i: (i, 0)),
                  pl.BlockSpec((1, lora), lambda i: (0, 0)),
                  pl.BlockSpec((tm, LANES), lambda i: (i, 0))],
        out_specs=[pl.BlockSpec((tm, lora + rope), lambda i: (i, 0)),
                   pl.BlockSpec((tm, wa), lambda i: (i, 0))],
        out_shape=[jax.ShapeDtypeStruct((M, lora + rope), F32),
                   jax.ShapeDtypeStruct((M, wa), BF16)],
        compiler_params=_cp("parallel"),
        name=name,
    )(a, g.reshape(1, lora), cs)


def _key_inv_rms(kb, waug, heads, nope, qk):
    n = kb.shape[0]
    kn = _dot_nt(waug, kb)
    sq = kn * kn
    ssh = jnp.sum(sq[:heads * nope].reshape(heads, nope, n), axis=1)
    ssr = jnp.sum(sq[heads * nope:], axis=0, keepdims=True)
    return lax.rsqrt((ssh + ssr) / qk + EPS)


def _rms_kernel(kb_ref, w_ref, o_ref, *, heads, nope, qk):
    o_ref[...] = _key_inv_rms(kb_ref[...], w_ref[...], heads, nope, qk)


def key_rms(kb, waug, heads, nope, qk, tk, name):
    M, W = kb.shape
    R = waug.shape[0]
    return pl.pallas_call(
        functools.partial(_rms_kernel, heads=heads, nope=nope, qk=qk),
        grid=(M // tk,),
        in_specs=[pl.BlockSpec((tk, W), lambda i: (i, 0)),
                  pl.BlockSpec((R, W), lambda i: (0, 0))],
        out_specs=pl.BlockSpec((heads, tk), lambda i: (0, i)),
        out_shape=jax.ShapeDtypeStruct((heads, M), F32),
        compiler_params=_cp("parallel"),
        name=name,
    )(kb, waug)


def _key_inv_rms_t(kt, wuk, heads, nope, lora, qk):
    n = kt.shape[1]
    kn = _dot(wuk, kt[:lora].astype(BF16))
    ssh = jnp.sum((kn * kn).reshape(heads, nope, n), axis=1)
    kr = kt[lora:]
    ssr = jnp.sum(kr * kr, axis=0, keepdims=True)
    return lax.rsqrt((ssh + ssr) / qk + EPS)


def _past_rms_kernel(pt_ref, *refs, npp, heads, nope, lora, qk, page, group):
    pages = refs[:npp]
    w_ref, o_ref = refs[npp:]
    for i in range(0, npp, group):
        kt = jnp.concatenate([pages[i + u][0] for u in range(group)], axis=1)
        o_ref[0, :, i * page:(i + group) * page] = _key_inv_rms_t(kt, w_ref[...], heads, nope, lora, qk)


def past_key_rms(cacheT, page_table, wuk2, heads, nope, qk, npp, name):
    nseq, npages = page_table.shape
    _, W, page = cacheT.shape
    R, lora = wuk2.shape
    nsteps = npages // npp
    group = 2 if npp % 2 == 0 else 1
    page_specs = [pl.BlockSpec((1, W, page), (lambda b, s, pt, i=i: (pt[b, s * npp + i], 0, 0)))
                  for i in range(npp)]
    return pl.pallas_call(
        functools.partial(_past_rms_kernel, npp=npp, heads=heads, nope=nope, lora=lora, qk=qk, page=page,
                          group=group),
        grid_spec=pltpu.PrefetchScalarGridSpec(
            num_scalar_prefetch=1,
            grid=(nseq, nsteps),
            in_specs=page_specs + [pl.BlockSpec((R, lora), lambda b, s, pt: (0, 0))],
            out_specs=pl.BlockSpec((1, heads, npp * page), lambda b, s, pt: (b, 0, s)),
        ),
        out_shape=jax.ShapeDtypeStruct((nseq, heads, npages * page), F32),
        compiler_params=_cp("parallel", "parallel"),
        name=name,
    )(page_table, *([cacheT] * npp), wuk2)


def _q_kernel(p_ref, g_ref, wq_ref, cs_ref, qn_ref, kn_ref, wuk_ref, o_ref, *, heads, nope, rope, lora, qk):
    p = p_ref[...]
    ms = jnp.mean(p * p, axis=-1, keepdims=True)
    qa = (p * lax.rsqrt(ms + EPS) * g_ref[...]).astype(BF16)
    lane = lax.broadcasted_iota(jnp.int32, (1, LANES), 1)
    cs = cs_ref[...]
    sc = qn_ref[...] * kn_ref[...] * (1.0 / math.sqrt(qk))
    hw = nope + LANES
    for h in range(heads):
        q = _dot(qa, wq_ref[:, h * hw:(h + 1) * hw])
        qnope = q[:, :nope]
        u = q[:, nope:] * cs
        qr = u + pltpu.roll(u, rope, 1)
        qr = jnp.where(lane < rope, qr, 0.0)
        ss = jnp.sum(qnope * qnope, axis=-1, keepdims=True) + jnp.sum(qr * qr, axis=-1, keepdims=True)
        inv = lax.rsqrt(ss / qk + EPS)
        qn = (qnope * inv * sc[:, :nope]).astype(BF16)
        o_ref[h, :, :lora] = _dot(qn, wuk_ref[h]).astype(BF16)
        o_ref[h, :, lora:lora + LANES] = (qr * inv * sc[:, nope:]).astype(BF16)


def q_side(proj, g, wq_ext, cs, qn_ext, kn_ext, wukT, dims, tm, name):
    heads, nope, rope, lora, qk = dims
    M = proj.shape[0]
    qlora = g.shape[0]
    hw = nope + LANES
    return pl.pallas_call(
        functools.partial(_q_kernel, heads=heads, nope=nope, rope=rope, lora=lora, qk=qk),
        grid=(M // tm,),
        in_specs=[pl.BlockSpec((tm, qlora), lambda i: (i, 0)),
                  pl.BlockSpec((1, qlora), lambda i: (0, 0)),
                  pl.BlockSpec((qlora, heads * hw), lambda i: (0, 0)),
                  pl.BlockSpec((tm, LANES), lambda i: (i, 0)),
                  pl.BlockSpec((1, hw), lambda i: (0, 0)),
                  pl.BlockSpec((1, hw), lambda i: (0, 0)),
                  pl.BlockSpec((heads, nope, lora), lambda i: (0, 0, 0))],
        out_specs=pl.BlockSpec((heads, tm, lora + LANES), lambda i: (0, i, 0)),
        out_shape=jax.ShapeDtypeStruct((heads, M, lora + LANES), BF16),
        compiler_params=_cp("parallel"),
        name=name,
    )(proj, g.reshape(1, qlora), wq_ext, cs, qn_ext, kn_ext, wukT)


def _softmax_update(s, v, m_sc, l_sc, acc_sc, rows=None, v_transposed=False):
    rows = slice(None) if rows is None else rows
    m_old = m_sc[rows]
    m_new = jnp.maximum(m_old, jnp.max(s, axis=-1, keepdims=True))
    alpha = jnp.exp(m_old - m_new)
    p = jnp.exp(s - m_new)
    l_sc[rows] = alpha * l_sc[rows] + jnp.sum(p, axis=-1, keepdims=True)
    pv = _dot_nt(p.astype(BF16), v) if v_transposed else _dot(p.astype(BF16), v)
    acc_sc[rows] = alpha * acc_sc[rows] + pv
    m_sc[rows] = m_new


def _attn_init(m_sc, l_sc, acc_sc):
    m_sc[...] = jnp.full_like(m_sc, NEG)
    l_sc[...] = jnp.zeros_like(l_sc)
    acc_sc[...] = jnp.zeros_like(acc_sc)


def _attn_finish(o_ref, wuv_ref, l_sc, acc_sc, heads, tq, vd):
    inv = 1.0 / l_sc[...]
    for h in range(heads):
        oh = (acc_sc[h * tq:(h + 1) * tq, :] * inv[h * tq:(h + 1) * tq]).astype(BF16)
        o_ref[0, :, h * vd:(h + 1) * vd] = _dot(oh, wuv_ref[h])


def _pattn_kernel(q_ref, k_ref, r_ref, wuv_ref, o_ref, m_sc, l_sc, acc_sc, *, hg, tq, ck, lora, vd):
    qi = pl.program_id(1)
    gi = pl.program_id(2)
    W = q_ref.shape[-1]
    Q = q_ref[:, 0].reshape(hg * tq, W)
    _attn_init(m_sc, l_sc, acc_sc)

    def chunk(c, masked):
        k0 = pl.multiple_of(c * ck, ck)
        Kc = k_ref[0, pl.ds(k0, ck), :]
        rr = r_ref[0, c, pl.ds(pl.multiple_of(gi * hg, hg), hg), :]
        s = _dot_nt(Q, Kc).reshape(hg, tq, ck) * rr[:, None, :]
        if masked:
            ti = qi * tq + lax.broadcasted_iota(jnp.int32, (1, tq, ck), 1)
            kj = k0 + lax.broadcasted_iota(jnp.int32, (1, tq, ck), 2)
            s = jnp.where(kj <= ti, s, NEG)
        _softmax_update(s.reshape(hg * tq, ck), Kc[:, :lora], m_sc, l_sc, acc_sc)

    nfull = (qi * tq) // ck

    def body(c, carry):
        chunk(c, False)
        return carry

    lax.fori_loop(0, nfull, body, 0)
    chunk(nfull, True)
    _attn_finish(o_ref, wuv_ref, l_sc, acc_sc, hg, tq, vd)


def prompt_attention(q4, kb3, r4, wuv, tq, ck, hg, name):
    heads, B, T, W = q4.shape
    Tk = kb3.shape[1]
    lora, vd = wuv.shape[1], wuv.shape[2]
    return pl.pallas_call(
        functools.partial(_pattn_kernel, hg=hg, tq=tq, ck=ck, lora=lora, vd=vd),
        grid=(B, T // tq, heads // hg),
        in_specs=[pl.BlockSpec((hg, 1, tq, W), lambda b, i, g: (g, b, i, 0)),
                  pl.BlockSpec((1, Tk, W), lambda b, i, g: (b, 0, 0)),
                  pl.BlockSpec((1, Tk // ck, heads, ck), lambda b, i, g: (b, 0, 0, 0)),
                  pl.BlockSpec((hg, lora, vd), lambda b, i, g: (g, 0, 0))],
        out_specs=pl.BlockSpec((1, tq, hg * vd), lambda b, i, g: (b, i, g)),
        out_shape=jax.ShapeDtypeStruct((B, T, heads * vd), F32),
        scratch_shapes=[pltpu.VMEM((hg * tq, 1), F32), pltpu.VMEM((hg * tq, 1), F32),
                        pltpu.VMEM((hg * tq, lora), F32)],
        compiler_params=_cp("parallel", "parallel", "arbitrary"),
        name=name,
    )(q4, kb3, r4, wuv)


def _sattn_kernel(pt_ref, q_ref, *refs, npp, nsteps, heads, tq, page, lora, vd):
    pages = refs[:npp]
    rp_ref, kn_ref, rn_ref, wuv_ref, o_ref, m_sc, l_sc, acc_sc, kt_sc = refs[npp:]
    step = pl.program_id(1)

    @pl.when(step == 0)
    def _():
        _attn_init(m_sc, l_sc, acc_sc)

    Q = q_ref[0]
    for i in range(npp):
        kt_sc[:, i * page:(i + 1) * page] = pages[i][0].astype(BF16)
    nk = npp * page
    s = _dot(Q, kt_sc[...]).reshape(heads, tq, nk) * rp_ref[0][:, None, :]
    _softmax_update(s.reshape(heads * tq, nk), kt_sc[:lora, :], m_sc, l_sc, acc_sc, v_transposed=True)

    @pl.when(step == nsteps - 1)
    def _():
        Kn = kn_ref[0]
        s = _dot(Q, Kn).reshape(heads, tq, page) * rn_ref[0][:, None, :]
        ti = lax.broadcasted_iota(jnp.int32, (1, tq, page), 1)
        kj = lax.broadcasted_iota(jnp.int32, (1, tq, page), 2)
        s = jnp.where(kj <= ti, s, NEG)
        _softmax_update(s.reshape(heads * tq, page), Kn[:lora, :], m_sc, l_sc, acc_sc, v_transposed=True)
        _attn_finish(o_ref, wuv_ref, l_sc, acc_sc, heads, tq, vd)


def sample_attention(page_table, q3, cacheT, r_past, knewT, rnew, wuv, tq, npp, name):
    nseq, npages = page_table.shape
    _, W, page = cacheT.shape
    heads, lora, vd = wuv.shape
    nsteps = npages // npp
    page_specs = [pl.BlockSpec((1, W, page), (lambda b, s, pt, i=i: (pt[b, s * npp + i], 0, 0)))
                  for i in range(npp)]
    return pl.pallas_call(
        functools.partial(_sattn_kernel, npp=npp, nsteps=nsteps, heads=heads, tq=tq, page=page,
                          lora=lora, vd=vd),
        grid_spec=pltpu.PrefetchScalarGridSpec(
            num_scalar_prefetch=1,
            grid=(nseq, nsteps),
            in_specs=[pl.BlockSpec((1, heads * tq, W), lambda b, s, pt: (b, 0, 0))] + page_specs + [
                pl.BlockSpec((1, heads, npp * page), lambda b, s, pt: (b, 0, s)),
                pl.BlockSpec((1, W, page), lambda b, s, pt: (b, 0, 0)),
                pl.BlockSpec((1, heads, page), lambda b, s, pt: (b, 0, 0)),
                pl.BlockSpec((heads, lora, vd), lambda b, s, pt: (0, 0, 0))],
            out_specs=pl.BlockSpec((1, tq, heads * vd), lambda b, s, pt: (b, 0, 0)),
            scratch_shapes=[pltpu.VMEM((heads * tq, 1), F32), pltpu.VMEM((heads * tq, 1), F32),
                            pltpu.VMEM((heads * tq, lora), F32), pltpu.VMEM((W, npp * page), BF16)],
        ),
        out_shape=jax.ShapeDtypeStruct((nseq, tq, heads * vd), F32),
        compiler_params=_cp("parallel", "arbitrary"),
        name=name,
    )(page_table, q3, *([cacheT] * npp), r_past, knewT, rnew, wuv)


def _rope_table(pos, rope):
    inv = 1.0 / (ROPE_BASE ** (jnp.arange(0, rope, 2, dtype=F32) / rope))
    f = pos.astype(F32)[:, None] * inv[None, :]
    emb = jnp.concatenate([f, f], -1)
    return jnp.concatenate([jnp.cos(emb), jnp.sin(emb)], -1)


def _rot_cols(w, rope):
    h = rope // 2
    return jnp.concatenate([-w[..., h:], w[..., :h]], -1)


def kernel(x_prompt, x_sample, state_ssm, state_conv, cache_kv, page_table, meta_tokens,
           a_norm, a_w_in, a_conv_w, a_conv_b, a_dt_bias, a_A_log, a_D, a_gate_norm, a_w_out,
           kv_norm, w_kv_a, kv_a_norm, w_uk, w_uv, k_norm,
           b_norm, b_w_in, b_q_a_norm, b_w_q, b_q_norm, b_w_out):
    bt, L, dm = x_prompt.shape
    nseq, S, _ = x_sample.shape
    n_meta = meta_tokens.shape[0]
    n_a = a_w_in.shape[0]
    n_b = b_w_in.shape[0]
    _, _, H, hd, N = state_ssm.shape
    cw, cdim = a_conv_w.shape[1:]
    d_inner = a_w_out.shape[1]
    G = (cdim - d_inner) // (2 * N)
    sdims = (H, hd, G, N)
    lora, heads, nope = w_uk.shape
    vd = w_uv.shape[2]
    kvrow = cache_kv.shape[2]
    rope = kvrow - lora
    qk = nope + rope
    qlora = b_q_a_norm.shape[1]
    page = cache_kv.shape[1]
    past = page_table.shape[1] * page
    tile = SUBLANES
    assert cw - 1 + S + 1 == tile and CHUNK % tile == 0 and (nseq * tile) % CHUNK == 0

    T = L + n_meta
    Tp = -(-T // CHUNK) * CHUNK
    Mp = bt * Tp
    Ms = nseq * S
    tmp = _pick(Mp, 512)
    tms = _pick(Ms, 512)

    xp = jnp.concatenate([jnp.broadcast_to(meta_tokens[None], (bt, n_meta, dm)), x_prompt,
                          jnp.zeros((bt, Tp - T, dm), F32)], 1).reshape(Mp, dm)
    xs = x_sample.reshape(Ms, dm)

    ssm_p, conv_p, conv_s = [], [], []
    s_new = None
    for i in range(n_a):
        w_main = a_w_in[i][:, :d_inner + cdim].astype(BF16)
        w_dt = a_w_in[i][:, d_inner + cdim:]
        w_out = a_w_out[i].astype(BF16)
        A = -jnp.exp(a_A_log[i].astype(F32))
        tn = _pick(d_inner + cdim, 1024)

        proj = norm_matmul(xp, a_norm[i], w_main, tmp, tn, name=f"a{i}_in_p")
        dt = dt_proj(xp, a_norm[i], w_dt, a_dt_bias[i], tmp, name=f"a{i}_dt_p")
        proj3 = proj.reshape(bt, Tp, d_inner + cdim)
        act = conv_silu(proj3, d_inner, a_conv_w[i], a_conv_b[i], 256,True, name=f"a{i}_conv_p")
        y, st = ssd_chunks(proj3, act, dt[:, :H].T, A, a_D[i], a_gate_norm[i], sdims,
                           seg=CHUNK, lo=0, hi=T, tile=None, name=f"a{i}_ssd_p")
        xp = matmul_res(y.reshape(Mp, d_inner), w_out, xp, tmp, dm, 512, name=f"a{i}_out_p")
        conv_p.append(proj3[:, T - (cw - 1):T, d_inner:])
        ssm_p.append(st.reshape(bt, G, N, H // G, hd).transpose(0, 1, 3, 4, 2).reshape(bt, H, hd, N))

        proj_s = norm_matmul(xs, a_norm[i], w_main, tms, tn, name=f"a{i}_in_s")
        dt_s = dt_proj(xs, a_norm[i], w_dt, a_dt_bias[i], tms, name=f"a{i}_dt_s")
        ps3 = proj_s.reshape(nseq, S, d_inner + cdim)
        z8 = jnp.concatenate([jnp.zeros((nseq, tile - S, d_inner), F32), ps3[..., :d_inner]], 1)
        full8 = jnp.concatenate([jnp.zeros((nseq, 1, cdim), F32), state_conv[i], ps3[..., d_inner:]], 1)
        conv_s.append(full8[:, tile - (cw - 1):])
        dt8T = jnp.concatenate([jnp.zeros((nseq, tile - S, H), F32), dt_s[:, :H].reshape(nseq, S, H)],
                               1).reshape(nseq * tile, H).T
        act8 = conv_silu(full8.reshape(1, nseq * tile, cdim), 0, a_conv_w[i], a_conv_b[i], 256,False,
                         name=f"a{i}_conv_s")
        ys, s_new = sample_state(state_ssm.reshape(n_a, nseq, d_inner, N), i, act8[0], dt8T, A, sdims,
                                 tile=tile, lo=tile - S, prev=s_new, name=f"a{i}_state_s")
        nblk = nseq * tile // CHUNK
        y8 = ssd_chunks(z8.reshape(nblk, CHUNK, d_inner), act8.reshape(nblk, CHUNK, cdim), dt8T, A, a_D[i],
                        a_gate_norm[i], sdims, seg=tile, lo=tile - S, hi=tile, tile=tile,
                        ystate=ys.reshape(nblk, CHUNK, d_inner), name=f"a{i}_ssd_s")
        y_s = y8.reshape(nseq, tile, d_inner)[:, tile - S:].reshape(Ms, d_inner)
        xs = matmul_res(y_s, w_out, xs, tms, dm, 512, name=f"a{i}_out_s")

    wa = lora + LANES
    w_kv_ext = jnp.concatenate([w_kv_a, _rot_cols(w_kv_a[:, lora:], rope)], 1).astype(BF16)
    cs_p = jnp.tile(_rope_table(jnp.arange(Tp), rope), (bt, 1))
    cs_s = jnp.tile(_rope_table(past + jnp.arange(S), rope), (nseq, 1))
    wukT2 = w_uk.transpose(1, 2, 0).reshape(heads * nope, lora)
    waug = jnp.zeros((heads * nope + rope, wa), F32)
    waug = waug.at[:heads * nope, :lora].set(wukT2)
    waug = waug.at[heads * nope:, lora:lora + rope].set(jnp.eye(rope, dtype=F32)).astype(BF16)
    wukT = w_uk.transpose(1, 2, 0).astype(BF16)
    wuv = w_uv.transpose(1, 0, 2).astype(BF16)

    a_p = norm_matmul(xp, kv_norm, w_kv_ext, tmp, wa, name="kv_a_p")
    rows_p, kb_p = kv_post(a_p, kv_a_norm, cs_p, lora, rope, tmp, name="kv_post_p")
    rT_p = key_rms(kb_p, waug, heads, nope, qk, _pick(Mp, 256), name="kv_rms_p")
    a_s = norm_matmul(xs, kv_norm, w_kv_ext, tms, wa, name="kv_a_s")
    rows_s, kb_s = kv_post(a_s, kv_a_norm, cs_s, lora, rope, tms, name="kv_post_s")
    rT_s = key_rms(kb_s, waug, heads, nope, qk, _pick(Ms, 256), name="kv_rms_s")
    npp = min(16, page_table.shape[1])
    cacheT = jnp.transpose(cache_kv, (0, 2, 1))
    r_past = past_key_rms(cacheT, page_table, wukT2.astype(BF16), heads, nope, qk, npp, name="kv_rms_past")

    ck = 2 * CHUNK
    Tk = -(-Tp // ck) * ck
    kb3 = jnp.pad(kb_p.reshape(bt, Tp, wa), ((0, 0), (0, Tk - Tp), (0, 0)))
    r4 = jnp.pad(rT_p.reshape(heads, bt, Tp), ((0, 0), (0, 0), (0, Tk - Tp)), constant_values=1.0)
    r4 = r4.reshape(heads, bt, Tk // ck, ck).transpose(1, 2, 0, 3)
    knew = jnp.pad(kb_s[:, :kvrow].reshape(nseq, S, kvrow), ((0, 0), (0, page - S), (0, 0))).transpose(0, 2, 1)
    rnew = jnp.pad(rT_s.reshape(heads, nseq, S).transpose(1, 0, 2), ((0, 0), (0, 0), (0, page - S)),
                   constant_values=1.0)

    def ext(v):
        return jnp.pad(v.astype(F32), (0, LANES - rope)).reshape(1, nope + LANES)

    qdims = (heads, nope, rope, lora, qk)
    tq = CHUNK
    for j in range(n_b):
        w_in = b_w_in[j].astype(BF16)
        wq3 = b_w_q[j].reshape(qlora, heads, qk)
        wq_ext = jnp.concatenate([wq3, _rot_cols(wq3[..., nope:], rope)], -1).reshape(qlora, heads * (nope + LANES))
        wq_ext = wq_ext.astype(BF16)
        w_out = b_w_out[j].astype(BF16)
        tn = _pick(w_in.shape[1], 1280)

        proj = norm_matmul(xp, b_norm[j], w_in, tmp, tn, name=f"b{j}_in_p")
        q = q_side(proj, b_q_a_norm[j], wq_ext, cs_p, ext(b_q_norm[j]), ext(k_norm), wukT, qdims,
                   _pick(Mp, 256), name=f"b{j}_q_p")
        o = prompt_attention(q.reshape(heads, bt, Tp, wa), kb3, r4, wuv, tq, ck, min(8, heads),
                             name=f"b{j}_attn_p")
        xp = matmul_res(o.reshape(Mp, heads * vd), w_out, xp, tmp, dm, 512, name=f"b{j}_out_p",
                        gate=proj, gate_col0=qlora)

        proj_s = norm_matmul(xs, b_norm[j], w_in, tms, tn, name=f"b{j}_in_s")
        q_s = q_side(proj_s, b_q_a_norm[j], wq_ext, cs_s, ext(b_q_norm[j]), ext(k_norm), wukT, qdims,
                     _pick(Ms, 256), name=f"b{j}_q_s")
        q8 = jnp.pad(q_s[..., :kvrow].reshape(heads, nseq, S, kvrow), ((0, 0), (0, 0), (0, tile - S), (0, 0)))
        q8 = q8.transpose(1, 0, 2, 3).reshape(nseq, heads * tile, kvrow)
        o_s = sample_attention(page_table, q8, cacheT, r_past, knew, rnew, wuv, tile, npp,
                               name=f"b{j}_attn_s")
        xs = matmul_res(o_s[:, :S].reshape(Ms, heads * vd), w_out, xs, tms, dm, 512, name=f"b{j}_out_s",
                        gate=proj_s, gate_col0=qlora)

    y_prompt = xp.reshape(bt, Tp, dm)[:, n_meta:T]
    kv_p = rows_p.reshape(bt, Tp, kvrow)[:, :T]
    return (y_prompt, xs.reshape(nseq, S, dm), jnp.stack(ssm_p), jnp.stack(conv_p), kv_p,
            s_new.reshape(n_a, nseq, H, hd, N), jnp.stack(conv_s), rows_s.reshape(nseq, S, kvrow))
```

```python
import functools
import math

import jax
import jax.numpy as jnp
from jax import lax
from jax.experimental import pallas as pl
from jax.experimental.pallas import tpu as pltpu

F32 = jnp.float32
BF16 = jnp.bfloat16
EPS = 1e-6
ROPE_BASE = 10000.0
LANES = 128
SUBLANES = 8
CHUNK = 128
NEG = -1e30
VMEM_LIMIT = 48 * 1024 * 1024


def _cp(*sem):
    return pltpu.CompilerParams(dimension_semantics=sem, vmem_limit_bytes=VMEM_LIMIT)


def _dot(a, b):
    return jnp.dot(a, b, preferred_element_type=F32)


def _dot_nt(a, b):
    return lax.dot_general(a, b, (((1,), (1,)), ((), ())), preferred_element_type=F32)


def _split3(a):
    h = a.astype(BF16)
    r = a - h.astype(F32)
    m = r.astype(BF16)
    l = (r - m.astype(F32)).astype(BF16)
    return h, m, l


def _silu(v):
    return v * jax.nn.sigmoid(v)


def _pick(n, target):
    best = None
    for t in range(8, min(n, target) + 1, 8):
        if n % t == 0:
            best = t
    assert best is not None, (n, target)
    return best


def _norm_matmul_kernel(x_ref, g_ref, w_ref, o_ref, xn_ref):
    @pl.when(pl.program_id(1) == 0)
    def _():
        x = x_ref[...]
        ms = jnp.mean(x * x, axis=-1, keepdims=True)
        xn_ref[...] = (x * lax.rsqrt(ms + EPS) * g_ref[...]).astype(BF16)

    o_ref[...] = _dot(xn_ref[...], w_ref[...])


def norm_matmul(x, g, w, tm, tn, name):
    M, K = x.shape
    N = w.shape[1]
    return pl.pallas_call(
        _norm_matmul_kernel,
        grid=(M // tm, N // tn),
        in_specs=[pl.BlockSpec((tm, K), lambda i, j: (i, 0)),
                  pl.BlockSpec((1, K), lambda i, j: (0, 0)),
                  pl.BlockSpec((K, tn), lambda i, j: (0, j))],
        out_specs=pl.BlockSpec((tm, tn), lambda i, j: (i, j)),
        out_shape=jax.ShapeDtypeStruct((M, N), F32),
        scratch_shapes=[pltpu.VMEM((tm, K), BF16)],
        compiler_params=_cp("parallel", "arbitrary"),
        name=name,
    )(x, g.reshape(1, K), w)


def _dt_kernel(x_ref, g_ref, wh_ref, wl_ref, b_ref, o_ref):
    x = x_ref[...]
    ms = jnp.mean(x * x, axis=-1, keepdims=True)
    xn = x * lax.rsqrt(ms + EPS) * g_ref[...]
    xh = xn.astype(BF16)
    xl = (xn - xh.astype(F32)).astype(BF16)
    v = _dot(xh, wh_ref[...]) + _dot(xl, wh_ref[...]) + _dot(xh, wl_ref[...]) + b_ref[...]
    o_ref[...] = jnp.maximum(v, 0.0) + jnp.log(1.0 + jnp.exp(-jnp.abs(v)))


def dt_proj(x, g, w_dt, bias, tm, name):
    M, K = x.shape
    H = w_dt.shape[1]
    Hp = -(-H // LANES) * LANES
    wp = jnp.pad(w_dt, ((0, 0), (0, Hp - H)))
    wh = wp.astype(BF16)
    wl = (wp - wh.astype(F32)).astype(BF16)
    bp = jnp.pad(bias.astype(F32), (0, Hp - H)).reshape(1, Hp)
    return pl.pallas_call(
        _dt_kernel,
        grid=(M // tm,),
        in_specs=[pl.BlockSpec((tm, K), lambda i: (i, 0)),
                  pl.BlockSpec((1, K), lambda i: (0, 0)),
                  pl.BlockSpec((K, Hp), lambda i: (0, 0)),
                  pl.BlockSpec((K, Hp), lambda i: (0, 0)),
                  pl.BlockSpec((1, Hp), lambda i: (0, 0))],
        out_specs=pl.BlockSpec((tm, Hp), lambda i: (i, 0)),
        out_shape=jax.ShapeDtypeStruct((M, Hp), F32),
        compiler_params=_cp("parallel"),
        name=name,
    )(x, g.reshape(1, K), wh, wl, bp)


def _mmres_kernel(*refs, gated, nk):
    if gated:
        a_ref, gate_ref, w_ref, r_ref, o_ref, acc = refs
    else:
        a_ref, w_ref, r_ref, o_ref, acc = refs
    k = pl.program_id(2)

    @pl.when(k == 0)
    def _():
        acc[...] = jnp.zeros_like(acc)

    a = a_ref[...]
    if gated:
        a = (a * _silu(gate_ref[...])).astype(BF16)
    acc[...] += _dot(a, w_ref[...])

    @pl.when(k == nk - 1)
    def _():
        o_ref[...] = r_ref[...] + acc[...]


def matmul_res(a, w, res, tm, tn, tk, name, gate=None, gate_col0=0):
    M, K = a.shape
    N = w.shape[1]
    nk = K // tk
    gated = gate is not None
    in_specs = [pl.BlockSpec((tm, tk), lambda i, j, k: (i, k))]
    args = [a]
    if gated:
        goff = gate_col0 // tk
        assert goff * tk == gate_col0
        in_specs.append(pl.BlockSpec((tm, tk), lambda i, j, k: (i, goff + k)))
        args.append(gate)
    in_specs += [pl.BlockSpec((tk, tn), lambda i, j, k: (k, j)),
                 pl.BlockSpec((tm, tn), lambda i, j, k: (i, j))]
    args += [w, res]
    return pl.pallas_call(
        functools.partial(_mmres_kernel, gated=gated, nk=nk),
        grid=(M // tm, N // tn, nk),
        in_specs=in_specs,
        out_specs=pl.BlockSpec((tm, tn), lambda i, j, k: (i, j)),
        out_shape=jax.ShapeDtypeStruct((M, N), F32),
        scratch_shapes=[pltpu.VMEM((tm, tn), F32)],
        compiler_params=_cp("parallel", "parallel", "arbitrary"),
        name=name,
    )(*args)


def _conv_kernel(x_ref, w_ref, b_ref, o_ref, *, width, zero_head):
    x = x_ref[0]
    rid = lax.broadcasted_iota(jnp.int32, x.shape, 0)
    acc = b_ref[...] + x * w_ref[width - 1:width, :]
    for k in range(width - 1):
        s = width - 1 - k
        xs = pltpu.roll(x, s, 0)
        if zero_head:
            xs = jnp.where(rid < s, 0.0, xs)
        acc = acc + xs * w_ref[k:k + 1, :]
    o_ref[0] = _silu(acc)


def conv_silu(inp, col0, w, b, tc, zero_head, name):
    Bt, T, _ = inp.shape
    W, C = w.shape
    off = col0 // tc
    assert off * tc == col0 and C % tc == 0
    return pl.pallas_call(
        functools.partial(_conv_kernel, width=W, zero_head=zero_head),
        grid=(Bt, C // tc),
        in_specs=[pl.BlockSpec((1, T, tc), lambda b, j: (b, 0, off + j)),
                  pl.BlockSpec((W, tc), lambda b, j: (0, j)),
                  pl.BlockSpec((1, tc), lambda b, j: (0, j))],
        out_specs=pl.BlockSpec((1, T, tc), lambda b, j: (b, 0, j)),
        out_shape=jax.ShapeDtypeStruct((Bt, T, C), F32),
        compiler_params=_cp("parallel", "parallel"),
        name=name,
    )(inp, w, b.reshape(1, C))


def _ssd_kernel(*refs, gpb, n_state, carry, hpg, hd, **kw):
    gw = hpg * hd

    def cols(r, w, u):
        return r.at[:, :, u * w:(u + 1) * w]

    for u in range(gpb):
        z_ref, x_ref, b_ref, c_ref, dtT_ref, acol_ref, aw_ref, dw_ref, gw_ref = refs[:9]
        views = [cols(z_ref, gw, u), cols(x_ref, gw, u), cols(b_ref, n_state, u), cols(c_ref, n_state, u),
                 dtT_ref.at[u * hpg:(u + 1) * hpg, :], acol_ref.at[u * hpg:(u + 1) * hpg, :],
                 aw_ref.at[:, u * gw:(u + 1) * gw], dw_ref.at[:, u * gw:(u + 1) * gw],
                 gw_ref.at[:, u * gw:(u + 1) * gw]]
        if carry:
            y_ref, st_ref, state = refs[9:]
            views += [cols(y_ref, gw, u), st_ref.at[:, u:u + 1], state.at[u]]
        else:
            ys_ref, y_ref = refs[9:]
            views += [cols(ys_ref, gw, u), cols(y_ref, gw, u)]
        _ssd_group(*views, carry=carry, hpg=hpg, hd=hd, **kw)


def _ssd_group(*refs, seg, lo, hi, tile, carry, hpg, hd, nchunk):
    if carry:
        (z_ref, x_ref, b_ref, c_ref, dtT_ref, acol_ref, aw_ref, dw_ref, gw_ref,
         y_ref, st_ref, state) = refs
    else:
        (z_ref, x_ref, b_ref, c_ref, dtT_ref, acol_ref, aw_ref, dw_ref, gw_ref,
         ys_ref, y_ref) = refs
    ci = pl.program_id(2)
    gw = hpg * hd
    x = x_ref[0]
    Bm = b_ref[0]
    Cm = c_ref[0]

    lane = lax.broadcasted_iota(jnp.int32, (1, CHUNK), 1)
    pos = ci * CHUNK + lane
    if tile is not None:
        pos = pos & (tile - 1)
    valid = (pos >= lo) & (pos < hi)
    dtT = jnp.where(valid, dtT_ref[...], 0.0)

    ii = lax.broadcasted_iota(jnp.int32, (CHUNK, CHUNK), 0)
    jj = lax.broadcasted_iota(jnp.int32, (CHUNK, CHUNK), 1)
    causal = jj <= ii
    upper = ii <= jj
    if seg < CHUNK:
        sh = int(math.log2(seg))
        same = (ii >> sh) == (jj >> sh)
        causal = causal & same
        upper = upper & same
    Lc = jnp.where(causal, 1.0, 0.0).astype(BF16)
    LT = jnp.where(upper, 1.0, 0.0).astype(BF16)
    eye = jnp.where(ii == jj, 1.0, 0.0).astype(BF16)

    parts = _split3(dtT)
    cum_row = sum(_dot(p, LT) for p in parts) * acol_ref[...]

    def widen(a):
        return jnp.broadcast_to(a[:, None, :], (hpg, hd, CHUNK)).reshape(gw, CHUNK)

    LI = jnp.concatenate([Lc, eye], axis=0)
    G = sum(_dot_nt(LI, widen(p.astype(F32)).astype(BF16)) for p in parts)
    cumcol = G[:CHUNK] * aw_ref[...]
    dtcol = G[CHUNK:]

    cb = _dot_nt(Cm.astype(BF16), Bm.astype(BF16))
    per = LANES // hd
    lanep = lax.broadcasted_iota(jnp.int32, (1, LANES), 1)
    ys = []
    for sp in range(gw // LANES):
        xp = x[:, sp * LANES:(sp + 1) * LANES]
        Ms, Xs = [], []
        for hh in range(per):
            h = sp * per + hh
            ccol = cumcol[:, h * hd:h * hd + 1]
            crow = cum_row[h:h + 1, :]
            dec = jnp.exp(jnp.where(causal, ccol - crow, NEG))
            Ms.append((dec * cb * dtT[h:h + 1, :]).astype(BF16))
            sel = (lanep >= hh * hd) & (lanep < (hh + 1) * hd)
            Xs.append(jnp.where(sel, xp, 0.0).astype(BF16))
        ys.append(_dot(jnp.concatenate(Ms, axis=1), jnp.concatenate(Xs, axis=0)))
    y = jnp.concatenate(ys, axis=1)

    if carry:
        @pl.when(ci == 0)
        def _():
            state[...] = jnp.zeros_like(state)

        ST = state[...]
        yst = _dot(Cm.astype(BF16), ST.astype(BF16))
    else:
        yst = ys_ref[0]
    y = y + yst * jnp.exp(cumcol)

    if carry:
        last = cumcol[CHUNK - 1:CHUNK, :]
        xw = (x * (jnp.exp(last - cumcol) * dtcol)).astype(BF16)
        new = ST * jnp.exp(last) + _dot(Bm.T.astype(BF16), xw)
        state[...] = new
        st_ref[0, 0] = new

    y = y + x * dw_ref[...]
    y = y * _silu(z_ref[0])
    ms = jnp.mean(y * y, axis=-1, keepdims=True)
    y_ref[0] = (y * lax.rsqrt(ms + EPS) * gw_ref[...]).astype(BF16)


def ssd_chunks(z3, act3, dtT, A, D, gate_norm, dims, *, seg, lo, hi, tile, ystate=None, name):
    H, hd, G, N = dims
    hpg = H // G
    gw = hpg * hd
    d_inner = H * hd
    Bt, T, _ = act3.shape
    nchunk = T // CHUNK
    carry = ystate is None
    gpb = 4 if G % 4 == 0 and (d_inner // N) % 4 == 0 else 1
    boff = d_inner // (N * gpb)
    coff = (d_inner + G * N) // (N * gpb)
    sh, sw, sn, ng = hpg * gpb, gw * gpb, N * gpb, G // gpb
    in_specs = [
        pl.BlockSpec((1, CHUNK, sw), lambda b, g, c: (b, c, g)),
        pl.BlockSpec((1, CHUNK, sw), lambda b, g, c: (b, c, g)),
        pl.BlockSpec((1, CHUNK, sn), lambda b, g, c: (b, c, boff + g)),
        pl.BlockSpec((1, CHUNK, sn), lambda b, g, c: (b, c, coff + g)),
        pl.BlockSpec((sh, CHUNK), lambda b, g, c: (g, b * nchunk + c)),
        pl.BlockSpec((sh, 1), lambda b, g, c: (g, 0)),
        pl.BlockSpec((1, sw), lambda b, g, c: (0, g)),
        pl.BlockSpec((1, sw), lambda b, g, c: (0, g)),
        pl.BlockSpec((1, sw), lambda b, g, c: (0, g)),
    ]
    args = [z3, act3, act3, act3, dtT, A.reshape(H, 1),
            jnp.repeat(A, hd).reshape(1, d_inner),
            jnp.repeat(D.astype(F32), hd).reshape(1, d_inner),
            gate_norm.astype(F32).reshape(1, d_inner)]
    y_spec = pl.BlockSpec((1, CHUNK, sw), lambda b, g, c: (b, c, g))
    y_shape = jax.ShapeDtypeStruct((Bt, T, d_inner), BF16)
    kern = functools.partial(_ssd_kernel, gpb=gpb, n_state=N, seg=seg, lo=lo, hi=hi, tile=tile, carry=carry,
                             hpg=hpg, hd=hd, nchunk=nchunk)
    if carry:
        return pl.pallas_call(
            kern, grid=(Bt, ng, nchunk), in_specs=in_specs,
            out_specs=[y_spec, pl.BlockSpec((1, gpb, N, gw), lambda b, g, c: (b, g, 0, 0))],
            out_shape=[y_shape, jax.ShapeDtypeStruct((Bt, G, N, gw), F32)],
            scratch_shapes=[pltpu.VMEM((gpb, N, gw), F32)],
            compiler_params=_cp("parallel", "parallel", "arbitrary"),
            name=name,
        )(*args)
    in_specs.append(pl.BlockSpec((1, CHUNK, sw), lambda b, g, c: (b, c, g)))
    args.append(ystate)
    return pl.pallas_call(
        kern, grid=(Bt, ng, nchunk), in_specs=in_specs, out_specs=y_spec, out_shape=y_shape,
        compiler_params=_cp("parallel", "parallel", "arbitrary"),
        name=name,
    )(*args)


def _sstate_kernel(*refs, H, hd, G, N, tile, lo, chained):
    if chained:
        s0_ref, act_ref, dtT_ref, acol_ref, _, ys_ref, sn_ref, xwT = refs
    else:
        s0_ref, act_ref, dtT_ref, acol_ref, ys_ref, sn_ref, xwT = refs
    d_inner = H * hd
    gw = d_inner // G
    per_blk = CHUNK // tile
    sh = int(math.log2(tile))
    sub = pl.program_id(0) % per_blk
    r0 = pl.multiple_of(sub * tile, tile)
    lane = lax.broadcasted_iota(jnp.int32, (1, CHUNK), 1)
    valid = (lane & (tile - 1)) >= lo
    dtT = jnp.where(valid, dtT_ref[...], 0.0)
    acol = acol_ref[...]

    @pl.when(sub == 0)
    def _():
        ii = lax.broadcasted_iota(jnp.int32, (CHUNK, CHUNK), 0)
        jj = lax.broadcasted_iota(jnp.int32, (CHUNK, CHUNK), 1)
        U = jnp.where((ii > jj) & ((ii >> sh) == (jj >> sh)), 1.0, 0.0).astype(BF16)
        suf = sum(_dot(p, U) for p in _split3(dtT)) * acol
        wd = jnp.exp(suf) * dtT
        wdw = jnp.broadcast_to(wd[:, None, :], (H, hd, CHUNK)).reshape(d_inner, CHUNK)
        xT = act_ref[:, 0:d_inner].T
        xwT[...] = (xT * wdw).astype(BF16)

    own = (lane >> sh) == sub
    last = jnp.sum(jnp.where(own, dtT, 0.0), axis=1, keepdims=True) * acol
    dec = jnp.exp(last)
    dec_col = jnp.broadcast_to(dec[:, None, :], (H, hd, 1)).reshape(d_inner, 1)
    rown = (lax.broadcasted_iota(jnp.int32, (CHUNK, 1), 0) >> sh) == sub
    for g in range(G):
        S0 = s0_ref[0, 0, g * gw:(g + 1) * gw, :]
        Cg = act_ref[pl.ds(r0, tile), d_inner + G * N + g * N:d_inner + G * N + (g + 1) * N]
        ys_ref[pl.ds(r0, tile), g * gw:(g + 1) * gw] = _dot_nt(Cg, S0)
        Bg = jnp.where(rown, act_ref[:, d_inner + g * N:d_inner + (g + 1) * N], 0.0).astype(BF16)
        dS = _dot(xwT[g * gw:(g + 1) * gw, :], Bg)
        sn_ref[0, 0, g * gw:(g + 1) * gw, :] = S0 * dec_col[g * gw:(g + 1) * gw] + dS


def sample_state(s_all, layer, act2, dtT, A, dims, *, tile, lo, prev, name):
    H, hd, G, N = dims
    d_inner = H * hd
    nseq = s_all.shape[1]
    rows, cdim = act2.shape
    per_blk = CHUNK // tile
    chained = prev is not None
    in_specs = [pl.BlockSpec((1, 1, d_inner, N), lambda b: (layer, b, 0, 0)),
                pl.BlockSpec((CHUNK, cdim), lambda b: (b // per_blk, 0)),
                pl.BlockSpec((H, CHUNK), lambda b: (0, b // per_blk)),
                pl.BlockSpec((H, 1), lambda b: (0, 0))]
    args = [s_all, act2, dtT, A.reshape(H, 1)]
    if chained:
        in_specs.append(pl.BlockSpec(memory_space=pl.ANY))
        args.append(prev)
    return pl.pallas_call(
        functools.partial(_sstate_kernel, H=H, hd=hd, G=G, N=N, tile=tile, lo=lo, chained=chained),
        grid=(nseq,),
        in_specs=in_specs,
        out_specs=[pl.BlockSpec((CHUNK, d_inner), lambda b: (b // per_blk, 0)),
                   pl.BlockSpec((1, 1, d_inner, N), lambda b: (layer, b, 0, 0))],
        out_shape=[jax.ShapeDtypeStruct((rows, d_inner), F32),
                   jax.ShapeDtypeStruct(s_all.shape, F32)],
        scratch_shapes=[pltpu.VMEM((d_inner, CHUNK), BF16)],
        input_output_aliases={4: 1} if chained else {},
        compiler_params=_cp("arbitrary"),
        name=name,
    )(*args)


def _kvpost_kernel(a_ref, g_ref, cs_ref, rows_ref, kb_ref, *, lora, rope):
    a = a_ref[...]
    c = a[:, :lora]
    ms = jnp.mean(c * c, axis=-1, keepdims=True)
    cn = c * lax.rsqrt(ms + EPS) * g_ref[...]
    u = a[:, lora:lora + LANES] * cs_ref[...]
    kr = u + pltpu.roll(u, rope, 1)
    lane = lax.broadcasted_iota(jnp.int32, (1, LANES), 1)
    rows_ref[:, :lora] = cn
    rows_ref[:, lora:lora + rope] = kr[:, :rope]
    kb_ref[:, :lora] = cn.astype(BF16)
    kb_ref[:, lora:lora + LANES] = jnp.where(lane < rope, kr, 0.0).astype(BF16)


def kv_post(a, g, cs, lora, rope, tm, name):
    M = a.shape[0]
    assert 2 * rope == LANES
    wa = lora + LANES
    return pl.pallas_call(
        functools.partial(_kvpost_kernel, lora=lora, rope=rope),
        grid=(M // tm,),
        in_specs=[pl.BlockSpec((tm, wa), lambda i: (i, 0)),
                  pl.BlockSpec((1, lora), lambda i: (0, 0)),
                  pl.BlockSpec((tm, LANES), lambda i: (i, 0))],
        out_specs=[pl.BlockSpec((tm, lora + rope), lambda i: (i, 0)),
                   pl.BlockSpec((tm, wa), lambda i: (i, 0))],
        out_shape=[jax.ShapeDtypeStruct((M, lora + rope), F32),
                   jax.ShapeDtypeStruct((M, wa), BF16)],
        compiler_params=_cp("parallel"),
        name=name,
    )(a, g.reshape(1, lora), cs)


def _key_inv_rms(kb, waug, heads, nope, qk):
    n = kb.shape[0]
    kn = _dot_nt(waug, kb)
    sq = kn * kn
    ssh = jnp.sum(sq[:heads * nope].reshape(heads, nope, n), axis=1)
    ssr = jnp.sum(sq[heads * nope:], axis=0, keepdims=True)
    return lax.rsqrt((ssh + ssr) / qk + EPS)


def _rms_kernel(kb_ref, w_ref, o_ref, *, heads, nope, qk):
    o_ref[...] = _key_inv_rms(kb_ref[...], w_ref[...], heads, nope, qk)


def key_rms(kb, waug, heads, nope, qk, tk, name):
    M, W = kb.shape
    R = waug.shape[0]
    return pl.pallas_call(
        functools.partial(_rms_kernel, heads=heads, nope=nope, qk=qk),
        grid=(M // tk,),
        in_specs=[pl.BlockSpec((tk, W), lambda i: (i, 0)),
                  pl.BlockSpec((R, W), lambda i: (0, 0))],
        out_specs=pl.BlockSpec((heads, tk), lambda i: (0, i)),
        out_shape=jax.ShapeDtypeStruct((heads, M), F32),
        compiler_params=_cp("parallel"),
        name=name,
    )(kb, waug)


def _q_kernel(p_ref, g_ref, wq_ref, cs_ref, qn_ref, kn_ref, wuk_ref, o_ref, *, heads, nope, rope, lora, qk):
    p = p_ref[...]
    ms = jnp.mean(p * p, axis=-1, keepdims=True)
    qa = (p * lax.rsqrt(ms + EPS) * g_ref[...]).astype(BF16)
    lane = lax.broadcasted_iota(jnp.int32, (1, LANES), 1)
    cs = cs_ref[...]
    sc = qn_ref[...] * kn_ref[...] * (1.0 / math.sqrt(qk))
    hw = nope + LANES
    for h in range(heads):
        q = _dot(qa, wq_ref[:, h * hw:(h + 1) * hw])
        qnope = q[:, :nope]
        u = q[:, nope:] * cs
        qr = u + pltpu.roll(u, rope, 1)
        qr = jnp.where(lane < rope, qr, 0.0)
        ss = jnp.sum(qnope * qnope, axis=-1, keepdims=True) + jnp.sum(qr * qr, axis=-1, keepdims=True)
        inv = lax.rsqrt(ss / qk + EPS)
        qn = (qnope * inv * sc[:, :nope]).astype(BF16)
        o_ref[h, :, :lora] = _dot(qn, wuk_ref[h]).astype(BF16)
        o_ref[h, :, lora:lora + LANES] = (qr * inv * sc[:, nope:]).astype(BF16)


def q_side(proj, g, wq_ext, cs, qn_ext, kn_ext, wukT, dims, tm, name):
    heads, nope, rope, lora, qk = dims
    M = proj.shape[0]
    qlora = g.shape[0]
    hw = nope + LANES
    return pl.pallas_call(
        functools.partial(_q_kernel, heads=heads, nope=nope, rope=rope, lora=lora, qk=qk),
        grid=(M // tm,),
        in_specs=[pl.BlockSpec((tm, qlora), lambda i: (i, 0)),
                  pl.BlockSpec((1, qlora), lambda i: (0, 0)),
                  pl.BlockSpec((qlora, heads * hw), lambda i: (0, 0)),
                  pl.BlockSpec((tm, LANES), lambda i: (i, 0)),
                  pl.BlockSpec((1, hw), lambda i: (0, 0)),
                  pl.BlockSpec((1, hw), lambda i: (0, 0)),
                  pl.BlockSpec((heads, nope, lora), lambda i: (0, 0, 0))],
        out_specs=pl.BlockSpec((heads, tm, lora + LANES), lambda i: (0, i, 0)),
        out_shape=jax.ShapeDtypeStruct((heads, M, lora + LANES), BF16),
        compiler_params=_cp("parallel"),
        name=name,
    )(proj, g.reshape(1, qlora), wq_ext, cs, qn_ext, kn_ext, wukT)


def _softmax_update(s, v, m_sc, l_sc, acc_sc, rows=None, v_transposed=False):
    rows = slice(None) if rows is None else rows
    m_old = m_sc[rows]
    m_new = jnp.maximum(m_old, jnp.max(s, axis=-1, keepdims=True))
    alpha = jnp.exp(m_old - m_new)
    p = jnp.exp(s - m_new)
    l_sc[rows] = alpha * l_sc[rows] + jnp.sum(p, axis=-1, keepdims=True)
    pv = _dot_nt(p.astype(BF16), v) if v_transposed else _dot(p.astype(BF16), v)
    acc_sc[rows] = alpha * acc_sc[rows] + pv
    m_sc[rows] = m_new


def _attn_init(m_sc, l_sc, acc_sc):
    m_sc[...] = jnp.full_like(m_sc, NEG)
    l_sc[...] = jnp.zeros_like(l_sc)
    acc_sc[...] = jnp.zeros_like(acc_sc)


def _attn_finish(o_ref, wuv_ref, l_sc, acc_sc, heads, tq, vd, head0=0):
    inv = 1.0 / l_sc[...]
    for h in range(heads):
        oh = (acc_sc[h * tq:(h + 1) * tq, :] * inv[h * tq:(h + 1) * tq]).astype(BF16)
        o_ref[0, :, (head0 + h) * vd:(head0 + h + 1) * vd] = _dot(oh, wuv_ref[head0 + h])


def _pattn_kernel(q_ref, k_ref, r_ref, wuv_ref, o_ref, *scratch, hg, nslab, tq, ck, lora, vd):
    qi = pl.program_id(1)
    gi = pl.program_id(2)
    W = q_ref.shape[-1]
    hs = hg // nslab
    slabs = [(scratch[u], scratch[nslab + u], scratch[2 * nslab + u]) for u in range(nslab)]
    for st in slabs:
        _attn_init(*st)

    def chunk(c, masked):
        k0 = pl.multiple_of(c * ck, ck)
        Kc = k_ref[0, pl.ds(k0, ck), :]
        if masked:
            ti = qi * tq + lax.broadcasted_iota(jnp.int32, (1, tq, ck), 1)
            kj = k0 + lax.broadcasted_iota(jnp.int32, (1, tq, ck), 2)
            keep = kj <= ti
        rg = r_ref[0, c, pl.ds(pl.multiple_of(gi * hg, hg), hg), :]
        for u, st in enumerate(slabs):
            Q = q_ref[u * hs:(u + 1) * hs, 0].reshape(hs * tq, W)
            rr = rg[u * hs:(u + 1) * hs]
            s = _dot_nt(Q, Kc).reshape(hs, tq, ck) * rr[:, None, :]
            if masked:
                s = jnp.where(keep, s, NEG)
            _softmax_update(s.reshape(hs * tq, ck), Kc[:, :lora], *st)

    nfull = (qi * tq) // ck

    def body(c, carry):
        chunk(c, False)
        return carry

    lax.fori_loop(0, nfull, body, 0)
    chunk(nfull, True)
    for u, (_, l_sc, acc_sc) in enumerate(slabs):
        _attn_finish(o_ref, wuv_ref, l_sc, acc_sc, hs, tq, vd, head0=u * hs)


def prompt_attention(q4, kb3, r4, wuv, tq, ck, hg, nslab, name):
    heads, B, T, W = q4.shape
    Tk = kb3.shape[1]
    lora, vd = wuv.shape[1], wuv.shape[2]
    rows = hg // nslab * tq
    return pl.pallas_call(
        functools.partial(_pattn_kernel, hg=hg, nslab=nslab, tq=tq, ck=ck, lora=lora, vd=vd),
        grid=(B, T // tq, heads // hg),
        in_specs=[pl.BlockSpec((hg, 1, tq, W), lambda b, i, g: (g, b, i, 0)),
                  pl.BlockSpec((1, Tk, W), lambda b, i, g: (b, 0, 0)),
                  pl.BlockSpec((1, Tk // ck, heads, ck), lambda b, i, g: (b, 0, 0, 0)),
                  pl.BlockSpec((hg, lora, vd), lambda b, i, g: (g, 0, 0))],
        out_specs=pl.BlockSpec((1, tq, hg * vd), lambda b, i, g: (b, i, g)),
        out_shape=jax.ShapeDtypeStruct((B, T, heads * vd), F32),
        scratch_shapes=([pltpu.VMEM((rows, 1), F32)] * (2 * nslab) + [pltpu.VMEM((rows, lora), F32)] * nslab),
        compiler_params=_cp("parallel", "parallel", "arbitrary"),
        name=name,
    )(q4, kb3, r4, wuv)


def _sattn_kernel(pt_ref, q_ref, *refs, npp, nsteps, heads, tq, page, lora, vd, nope, qk, group, make_r):
    pages = refs[:npp]
    if make_r:
        wuk_ref, kn_ref, rn_ref, wuv_ref, o_ref, rp_ref, m_sc, l_sc, acc_sc, kt_sc = refs[npp:]
    else:
        rp_ref, kn_ref, rn_ref, wuv_ref, o_ref, m_sc, l_sc, acc_sc, kt_sc = refs[npp:]
    step = pl.program_id(1)

    @pl.when(step == 0)
    def _():
        _attn_init(m_sc, l_sc, acc_sc)

    Q = q_ref[0]
    ssr = []
    for i in range(npp):
        pg = pages[i][0]
        kt_sc[:, i * page:(i + 1) * page] = pg.astype(BF16)
        if make_r:
            kr = pg[lora:]
            ssr.append(jnp.sum(kr * kr, axis=0, keepdims=True))
    nk = npp * page
    if make_r:
        for g in range(0, npp, group):
            n0, n1 = g * page, (g + group) * page
            kn = _dot(wuk_ref[...], kt_sc[:lora, n0:n1])
            ssh = jnp.sum((kn * kn).reshape(heads, nope, n1 - n0), axis=1)
            rp_ref[0, :, n0:n1] = lax.rsqrt((ssh + jnp.concatenate(ssr[g:g + group], axis=1)) / qk + EPS)
    s = _dot(Q, kt_sc[...]).reshape(heads, tq, nk) * rp_ref[0][:, None, :]
    _softmax_update(s.reshape(heads * tq, nk), kt_sc[:lora, :], m_sc, l_sc, acc_sc, v_transposed=True)

    @pl.when(step == nsteps - 1)
    def _():
        Kn = kn_ref[0]
        s = _dot(Q, Kn).reshape(heads, tq, page) * rn_ref[0][:, None, :]
        ti = lax.broadcasted_iota(jnp.int32, (1, tq, page), 1)
        kj = lax.broadcasted_iota(jnp.int32, (1, tq, page), 2)
        s = jnp.where(kj <= ti, s, NEG)
        _softmax_update(s.reshape(heads * tq, page), Kn[:lora, :], m_sc, l_sc, acc_sc, v_transposed=True)
        _attn_finish(o_ref, wuv_ref, l_sc, acc_sc, heads, tq, vd)


def sample_attention(page_table, q3, cacheT, r_or_wuk, knewT, rnew, wuv, tq, npp, nope, qk, make_r, name):
    nseq, npages = page_table.shape
    _, W, page = cacheT.shape
    heads, lora, vd = wuv.shape
    nsteps = npages // npp
    group = 2 if npp % 2 == 0 else 1
    page_specs = [pl.BlockSpec((1, W, page), (lambda b, s, pt, i=i: (pt[b, s * npp + i], 0, 0)))
                  for i in range(npp)]
    r_spec = pl.BlockSpec((1, heads, npp * page), lambda b, s, pt: (b, 0, s))
    o_spec = pl.BlockSpec((1, tq, heads * vd), lambda b, s, pt: (b, 0, 0))
    o_shape = jax.ShapeDtypeStruct((nseq, tq, heads * vd), F32)
    if make_r:
        first = pl.BlockSpec(r_or_wuk.shape, lambda b, s, pt: (0, 0))
        out_specs = [o_spec, r_spec]
        out_shape = [o_shape, jax.ShapeDtypeStruct((nseq, heads, npages * page), F32)]
    else:
        first = r_spec
        out_specs, out_shape = o_spec, o_shape
    return pl.pallas_call(
        functools.partial(_sattn_kernel, npp=npp, nsteps=nsteps, heads=heads, tq=tq, page=page,
                          lora=lora, vd=vd, nope=nope, qk=qk, group=group, make_r=make_r),
        grid_spec=pltpu.PrefetchScalarGridSpec(
            num_scalar_prefetch=1,
            grid=(nseq, nsteps),
            in_specs=[pl.BlockSpec((1, heads * tq, W), lambda b, s, pt: (b, 0, 0))] + page_specs + [
                first,
                pl.BlockSpec((1, W, page), lambda b, s, pt: (b, 0, 0)),
                pl.BlockSpec((1, heads, page), lambda b, s, pt: (b, 0, 0)),
                pl.BlockSpec((heads, lora, vd), lambda b, s, pt: (0, 0, 0))],
            out_specs=out_specs,
            scratch_shapes=[pltpu.VMEM((heads * tq, 1), F32), pltpu.VMEM((heads * tq, 1), F32),
                            pltpu.VMEM((heads * tq, lora), F32), pltpu.VMEM((W, npp * page), BF16)],
        ),
        out_shape=out_shape,
        compiler_params=_cp("parallel", "arbitrary"),
        name=name,
    )(page_table, q3, *([cacheT] * npp), r_or_wuk, knewT, rnew, wuv)


def _rope_table(pos, rope):
    inv = 1.0 / (ROPE_BASE ** (jnp.arange(0, rope, 2, dtype=F32) / rope))
    f = pos.astype(F32)[:, None] * inv[None, :]
    emb = jnp.concatenate([f, f], -1)
    return jnp.concatenate([jnp.cos(emb), jnp.sin(emb)], -1)


def _rot_cols(w, rope):
    h = rope // 2
    return jnp.concatenate([-w[..., h:], w[..., :h]], -1)


def kernel(x_prompt, x_sample, state_ssm, state_conv, cache_kv, page_table, meta_tokens,
           a_norm, a_w_in, a_conv_w, a_conv_b, a_dt_bias, a_A_log, a_D, a_gate_norm, a_w_out,
           kv_norm, w_kv_a, kv_a_norm, w_uk, w_uv, k_norm,
           b_norm, b_w_in, b_q_a_norm, b_w_q, b_q_norm, b_w_out):
    bt, L, dm = x_prompt.shape
    nseq, S, _ = x_sample.shape
    n_meta = meta_tokens.shape[0]
    n_a = a_w_in.shape[0]
    n_b = b_w_in.shape[0]
    _, _, H, hd, N = state_ssm.shape
    cw, cdim = a_conv_w.shape[1:]
    d_inner = a_w_out.shape[1]
    G = (cdim - d_inner) // (2 * N)
    sdims = (H, hd, G, N)
    lora, heads, nope = w_uk.shape
    vd = w_uv.shape[2]
    kvrow = cache_kv.shape[2]
    rope = kvrow - lora
    qk = nope + rope
    qlora = b_q_a_norm.shape[1]
    page = cache_kv.shape[1]
    past = page_table.shape[1] * page
    tile = SUBLANES
    assert cw - 1 + S + 1 == tile and CHUNK % tile == 0 and (nseq * tile) % CHUNK == 0

    T = L + n_meta
    Tp = -(-T // CHUNK) * CHUNK
    Mp = bt * Tp
    Ms = nseq * S
    tmp = _pick(Mp, 512)
    tms = _pick(Ms, 512)

    xp = jnp.concatenate([jnp.broadcast_to(meta_tokens[None], (bt, n_meta, dm)), x_prompt,
                          jnp.zeros((bt, Tp - T, dm), F32)], 1).reshape(Mp, dm)
    xs = x_sample.reshape(Ms, dm)

    ssm_p, conv_p, conv_s = [], [], []
    s_new = None
    for i in range(n_a):
        w_main = a_w_in[i][:, :d_inner + cdim].astype(BF16)
        w_dt = a_w_in[i][:, d_inner + cdim:]
        w_out = a_w_out[i].astype(BF16)
        A = -jnp.exp(a_A_log[i].astype(F32))
        tn = _pick(d_inner + cdim, 512)
        tmw = _pick(Mp, 1088)

        proj = norm_matmul(xp, a_norm[i], w_main, tmw, tn, name=f"a{i}_in_p")
        dt = dt_proj(xp, a_norm[i], w_dt, a_dt_bias[i], tmp, name=f"a{i}_dt_p")
        proj3 = proj.reshape(bt, Tp, d_inner + cdim)
        act = conv_silu(proj3, d_inner, a_conv_w[i], a_conv_b[i], 256,True, name=f"a{i}_conv_p")
        y, st = ssd_chunks(proj3, act, dt[:, :H].T, A, a_D[i], a_gate_norm[i], sdims,
                           seg=CHUNK, lo=0, hi=T, tile=None, name=f"a{i}_ssd_p")
        xp = matmul_res(y.reshape(Mp, d_inner), w_out, xp, tmw, 512, d_inner, name=f"a{i}_out_p")
        conv_p.append(proj3[:, T - (cw - 1):T, d_inner:])
        ssm_p.append(st.reshape(bt, G, N, H // G, hd).transpose(0, 1, 3, 4, 2).reshape(bt, H, hd, N))

        proj_s = norm_matmul(xs, a_norm[i], w_main, tms, tn, name=f"a{i}_in_s")
        dt_s = dt_proj(xs, a_norm[i], w_dt, a_dt_bias[i], tms, name=f"a{i}_dt_s")
        ps3 = proj_s.reshape(nseq, S, d_inner + cdim)
        z8 = jnp.concatenate([jnp.zeros((nseq, tile - S, d_inner), F32), ps3[..., :d_inner]], 1)
        full8 = jnp.concatenate([jnp.zeros((nseq, 1, cdim), F32), state_conv[i], ps3[..., d_inner:]], 1)
        conv_s.append(full8[:, tile - (cw - 1):])
        dt8T = jnp.concatenate([jnp.zeros((nseq, tile - S, H), F32), dt_s[:, :H].reshape(nseq, S, H)],
                               1).reshape(nseq * tile, H).T
        act8 = conv_silu(full8.reshape(1, nseq * tile, cdim), 0, a_conv_w[i], a_conv_b[i], 256,False,
                         name=f"a{i}_conv_s")
        ys, s_new = sample_state(state_ssm.reshape(n_a, nseq, d_inner, N), i, act8[0], dt8T, A, sdims,
                                 tile=tile, lo=tile - S, prev=s_new, name=f"a{i}_state_s")
        nblk = nseq * tile // CHUNK
        y8 = ssd_chunks(z8.reshape(nblk, CHUNK, d_inner), act8.reshape(nblk, CHUNK, cdim), dt8T, A, a_D[i],
                        a_gate_norm[i], sdims, seg=tile, lo=tile - S, hi=tile, tile=tile,
                        ystate=ys.reshape(nblk, CHUNK, d_inner), name=f"a{i}_ssd_s")
        y_s = y8.reshape(nseq, tile, d_inner)[:, tile - S:].reshape(Ms, d_inner)
        xs = matmul_res(y_s, w_out, xs, tms, dm, 512, name=f"a{i}_out_s")

    wa = lora + LANES
    w_kv_ext = jnp.concatenate([w_kv_a, _rot_cols(w_kv_a[:, lora:], rope)], 1).astype(BF16)
    cs_p = jnp.tile(_rope_table(jnp.arange(Tp), rope), (bt, 1))
    cs_s = jnp.tile(_rope_table(past + jnp.arange(S), rope), (nseq, 1))
    wukT2 = w_uk.transpose(1, 2, 0).reshape(heads * nope, lora)
    waug = jnp.zeros((heads * nope + rope, wa), F32)
    waug = waug.at[:heads * nope, :lora].set(wukT2)
    waug = waug.at[heads * nope:, lora:lora + rope].set(jnp.eye(rope, dtype=F32)).astype(BF16)
    wukT = w_uk.transpose(1, 2, 0).astype(BF16)
    wuv = w_uv.transpose(1, 0, 2).astype(BF16)

    a_p = norm_matmul(xp, kv_norm, w_kv_ext, tmp, wa, name="kv_a_p")
    rows_p, kb_p = kv_post(a_p, kv_a_norm, cs_p, lora, rope, tmp, name="kv_post_p")
    rT_p = key_rms(kb_p, waug, heads, nope, qk, _pick(Mp, 256), name="kv_rms_p")
    a_s = norm_matmul(xs, kv_norm, w_kv_ext, tms, wa, name="kv_a_s")
    rows_s, kb_s = kv_post(a_s, kv_a_norm, cs_s, lora, rope, tms, name="kv_post_s")
    rT_s = key_rms(kb_s, waug, heads, nope, qk, _pick(Ms, 256), name="kv_rms_s")
    npp = min(16, page_table.shape[1])
    cacheT = jnp.transpose(cache_kv, (0, 2, 1))
    r_past = None

    ck = 4 * CHUNK
    Tk = -(-Tp // ck) * ck
    kb3 = jnp.pad(kb_p.reshape(bt, Tp, wa), ((0, 0), (0, Tk - Tp), (0, 0)))
    r4 = jnp.pad(rT_p.reshape(heads, bt, Tp), ((0, 0), (0, 0), (0, Tk - Tp)), constant_values=1.0)
    r4 = r4.reshape(heads, bt, Tk // ck, ck).transpose(1, 2, 0, 3)
    knew = jnp.pad(kb_s[:, :kvrow].reshape(nseq, S, kvrow), ((0, 0), (0, page - S), (0, 0))).transpose(0, 2, 1)
    rnew = jnp.pad(rT_s.reshape(heads, nseq, S).transpose(1, 0, 2), ((0, 0), (0, 0), (0, page - S)),
                   constant_values=1.0)

    def ext(v):
        return jnp.pad(v.astype(F32), (0, LANES - rope)).reshape(1, nope + LANES)

    qdims = (heads, nope, rope, lora, qk)
    tq = CHUNK
    for j in range(n_b):
        w_in = b_w_in[j].astype(BF16)
        wq3 = b_w_q[j].reshape(qlora, heads, qk)
        wq_ext = jnp.concatenate([wq3, _rot_cols(wq3[..., nope:], rope)], -1).reshape(qlora, heads * (nope + LANES))
        wq_ext = wq_ext.astype(BF16)
        w_out = b_w_out[j].astype(BF16)
        tn = _pick(w_in.shape[1], 1280)

        proj = norm_matmul(xp, b_norm[j], w_in, tmp, tn, name=f"b{j}_in_p")
        q = q_side(proj, b_q_a_norm[j], wq_ext, cs_p, ext(b_q_norm[j]), ext(k_norm), wukT, qdims,
                   _pick(Mp, 256), name=f"b{j}_q_p")
        o = prompt_attention(q.reshape(heads, bt, Tp, wa), kb3, r4, wuv, tq, ck, min(16, heads), 1,
                             name=f"b{j}_attn_p")
        xp = matmul_res(o.reshape(Mp, heads * vd), w_out, xp, tmp, dm, 512, name=f"b{j}_out_p",
                        gate=proj, gate_col0=qlora)

        proj_s = norm_matmul(xs, b_norm[j], w_in, tms, tn, name=f"b{j}_in_s")
        q_s = q_side(proj_s, b_q_a_norm[j], wq_ext, cs_s, ext(b_q_norm[j]), ext(k_norm), wukT, qdims,
                     _pick(Ms, 256), name=f"b{j}_q_s")
        q8 = jnp.pad(q_s[..., :kvrow].reshape(heads, nseq, S, kvrow), ((0, 0), (0, 0), (0, tile - S), (0, 0)))
        q8 = q8.transpose(1, 0, 2, 3).reshape(nseq, heads * tile, kvrow)
        if r_past is None:
            o_s, r_past = sample_attention(page_table, q8, cacheT, wukT2.astype(BF16), knew, rnew, wuv, tile, npp,
                                           nope, qk, True, name=f"b{j}_attn_s")
        else:
            o_s = sample_attention(page_table, q8, cacheT, r_past, knew, rnew, wuv, tile, npp,
                                   nope, qk, False, name=f"b{j}_attn_s")
        xs = matmul_res(o_s[:, :S].reshape(Ms, heads * vd), w_out, xs, tms, dm, 512, name=f"b{j}_out_s",
                        gate=proj_s, gate_col0=qlora)

    y_prompt = xp.reshape(bt, Tp, dm)[:, n_meta:T]
    kv_p = rows_p.reshape(bt, Tp, kvrow)[:, :T]
    return (y_prompt, xs.reshape(nseq, S, dm), jnp.stack(ssm_p), jnp.stack(conv_p), kv_p,
            s_new.reshape(n_a, nseq, H, hd, N), jnp.stack(conv_s), rows_s.reshape(nseq, S, kvrow))
```

```python
import functools
import math

import jax
import jax.numpy as jnp
from jax import lax
from jax.experimental import pallas as pl
from jax.experimental.pallas import tpu as pltpu

F32 = jnp.float32
BF16 = jnp.bfloat16
EPS = 1e-6
ROPE_BASE = 10000.0
LANES = 128
SUBLANES = 8
CHUNK = 128
NEG = -1e30
VMEM_LIMIT = 48 * 1024 * 1024


def _cp(*sem):
    return pltpu.CompilerParams(dimension_semantics=sem, vmem_limit_bytes=VMEM_LIMIT)


def _dot(a, b):
    return jnp.dot(a, b, preferred_element_type=F32)


def _dot_nt(a, b):
    return lax.dot_general(a, b, (((1,), (1,)), ((), ())), preferred_element_type=F32)


def _split3(a):
    h = a.astype(BF16)
    r = a - h.astype(F32)
    m = r.astype(BF16)
    l = (r - m.astype(F32)).astype(BF16)
    return h, m, l


def _silu(v):
    return v * jax.nn.sigmoid(v)


def _pick(n, target):
    best = None
    for t in range(8, min(n, target) + 1, 8):
        if n % t == 0:
            best = t
    assert best is not None, (n, target)
    return best


def _norm_matmul_kernel(x_ref, g_ref, w_ref, o_ref, xn_ref):
    @pl.when(pl.program_id(1) == 0)
    def _():
        x = x_ref[...]
        ms = jnp.mean(x * x, axis=-1, keepdims=True)
        xn_ref[...] = (x * lax.rsqrt(ms + EPS) * g_ref[...]).astype(BF16)

    o_ref[...] = _dot(xn_ref[...], w_ref[...])


def norm_matmul(x, g, w, tm, tn, name):
    M, K = x.shape
    N = w.shape[1]
    return pl.pallas_call(
        _norm_matmul_kernel,
        grid=(M // tm, N // tn),
        in_specs=[pl.BlockSpec((tm, K), lambda i, j: (i, 0)),
                  pl.BlockSpec((1, K), lambda i, j: (0, 0)),
                  pl.BlockSpec((K, tn), lambda i, j: (0, j))],
        out_specs=pl.BlockSpec((tm, tn), lambda i, j: (i, j)),
        out_shape=jax.ShapeDtypeStruct((M, N), F32),
        scratch_shapes=[pltpu.VMEM((tm, K), BF16)],
        compiler_params=_cp("parallel", "arbitrary"),
        name=name,
    )(x, g.reshape(1, K), w)


def _norm_matmul_t_kernel(x_ref, g_ref, w_ref, o_ref, xn_ref):
    @pl.when(pl.program_id(1) == 0)
    def _():
        x = x_ref[...]
        ms = jnp.mean(x * x, axis=-1, keepdims=True)
        xn_ref[...] = (x * lax.rsqrt(ms + EPS) * g_ref[...]).astype(BF16)

    o_ref[...] = _dot_nt(xn_ref[...], w_ref[0].astype(BF16))


def norm_matmul_t(x, g, wT, layer, n_cols, tm, tn, name):
    M, K = x.shape
    assert n_cols % tn == 0
    return pl.pallas_call(
        _norm_matmul_t_kernel,
        grid=(M // tm, n_cols // tn),
        in_specs=[pl.BlockSpec((tm, K), lambda i, j: (i, 0)),
                  pl.BlockSpec((1, K), lambda i, j: (0, 0)),
                  pl.BlockSpec((1, tn, K), lambda i, j: (layer, j, 0))],
        out_specs=pl.BlockSpec((tm, tn), lambda i, j: (i, j)),
        out_shape=jax.ShapeDtypeStruct((M, n_cols), F32),
        scratch_shapes=[pltpu.VMEM((tm, K), BF16)],
        compiler_params=_cp("parallel", "arbitrary"),
        name=name,
    )(x, g.reshape(1, K), wT)


def _dt_kernel(x_ref, g_ref, w_ref, b_ref, o_ref):
    x = x_ref[...]
    ms = jnp.mean(x * x, axis=-1, keepdims=True)
    xn = x * lax.rsqrt(ms + EPS) * g_ref[...]
    xh = xn.astype(BF16)
    xl = (xn - xh.astype(F32)).astype(BF16)
    w = w_ref[0]
    wh = w.astype(BF16)
    wl = (w - wh.astype(F32)).astype(BF16)
    v = _dot_nt(xh, wh) + _dot_nt(xl, wh) + _dot_nt(xh, wl) + b_ref[...]
    o_ref[...] = jnp.maximum(v, 0.0) + jnp.log(1.0 + jnp.exp(-jnp.abs(v)))


def dt_proj(x, g, wT, layer, col0, bias, tm, name):
    M, K = x.shape
    H = bias.shape[0]
    assert col0 % H == 0 and wT.shape[1] == col0 + H
    return pl.pallas_call(
        _dt_kernel,
        grid=(M // tm,),
        in_specs=[pl.BlockSpec((tm, K), lambda i: (i, 0)),
                  pl.BlockSpec((1, K), lambda i: (0, 0)),
                  pl.BlockSpec((1, H, K), lambda i: (layer, col0 // H, 0)),
                  pl.BlockSpec((1, H), lambda i: (0, 0))],
        out_specs=pl.BlockSpec((tm, H), lambda i: (i, 0)),
        out_shape=jax.ShapeDtypeStruct((M, H), F32),
        compiler_params=_cp("parallel"),
        name=name,
    )(x, g.reshape(1, K), wT, bias.astype(F32).reshape(1, H))


def _mmres_kernel(*refs, gated, nk):
    if gated:
        a_ref, gate_ref, w_ref, r_ref, o_ref, acc = refs
    else:
        a_ref, w_ref, r_ref, o_ref, acc = refs
    k = pl.program_id(2)

    @pl.when(k == 0)
    def _():
        acc[...] = jnp.zeros_like(acc)

    a = a_ref[...]
    if gated:
        a = (a * _silu(gate_ref[...])).astype(BF16)
    acc[...] += _dot(a, w_ref[...])

    @pl.when(k == nk - 1)
    def _():
        o_ref[...] = r_ref[...] + acc[...]


def matmul_res(a, w, res, tm, tn, tk, name, gate=None, gate_col0=0):
    M, K = a.shape
    N = w.shape[1]
    nk = K // tk
    gated = gate is not None
    in_specs = [pl.BlockSpec((tm, tk), lambda i, j, k: (i, k))]
    args = [a]
    if gated:
        goff = gate_col0 // tk
        assert goff * tk == gate_col0
        in_specs.append(pl.BlockSpec((tm, tk), lambda i, j, k: (i, goff + k)))
        args.append(gate)
    in_specs += [pl.BlockSpec((tk, tn), lambda i, j, k: (k, j)),
                 pl.BlockSpec((tm, tn), lambda i, j, k: (i, j))]
    args += [w, res]
    return pl.pallas_call(
        functools.partial(_mmres_kernel, gated=gated, nk=nk),
        grid=(M // tm, N // tn, nk),
        in_specs=in_specs,
        out_specs=pl.BlockSpec((tm, tn), lambda i, j, k: (i, j)),
        out_shape=jax.ShapeDtypeStruct((M, N), F32),
        scratch_shapes=[pltpu.VMEM((tm, tn), F32)],
        compiler_params=_cp("parallel", "parallel", "arbitrary"),
        name=name,
    )(*args)


def _conv_kernel(x_ref, w_ref, b_ref, o_ref, *, width, zero_head):
    x = x_ref[0]
    rid = lax.broadcasted_iota(jnp.int32, x.shape, 0)
    acc = b_ref[...] + x * w_ref[width - 1:width, :]
    for k in range(width - 1):
        s = width - 1 - k
        xs = pltpu.roll(x, s, 0)
        if zero_head:
            xs = jnp.where(rid < s, 0.0, xs)
        acc = acc + xs * w_ref[k:k + 1, :]
    o_ref[0] = _silu(acc)


def conv_silu(inp, col0, w, b, tc, zero_head, name):
    Bt, T, _ = inp.shape
    W, C = w.shape
    off = col0 // tc
    assert off * tc == col0 and C % tc == 0
    return pl.pallas_call(
        functools.partial(_conv_kernel, width=W, zero_head=zero_head),
        grid=(Bt, C // tc),
        in_specs=[pl.BlockSpec((1, T, tc), lambda b, j: (b, 0, off + j)),
                  pl.BlockSpec((W, tc), lambda b, j: (0, j)),
                  pl.BlockSpec((1, tc), lambda b, j: (0, j))],
        out_specs=pl.BlockSpec((1, T, tc), lambda b, j: (b, 0, j)),
        out_shape=jax.ShapeDtypeStruct((Bt, T, C), F32),
        compiler_params=_cp("parallel", "parallel"),
        name=name,
    )(inp, w, b.reshape(1, C))


def _ssd_kernel(*refs, gpb, n_state, carry, hpg, hd, **kw):
    gw = hpg * hd

    def cols(r, w, u):
        return r.at[:, :, u * w:(u + 1) * w]

    for u in range(gpb):
        z_ref, x_ref, b_ref, c_ref, dtT_ref, acol_ref, aw_ref, dw_ref, gw_ref = refs[:9]
        views = [cols(z_ref, gw, u), cols(x_ref, gw, u), cols(b_ref, n_state, u), cols(c_ref, n_state, u),
                 dtT_ref.at[u * hpg:(u + 1) * hpg, :], acol_ref.at[u * hpg:(u + 1) * hpg, :],
                 aw_ref.at[:, u * gw:(u + 1) * gw], dw_ref.at[:, u * gw:(u + 1) * gw],
                 gw_ref.at[:, u * gw:(u + 1) * gw]]
        if carry:
            y_ref, st_ref, state = refs[9:]
            views += [cols(y_ref, gw, u), st_ref.at[:, u:u + 1], state.at[u]]
        else:
            ys_ref, y_ref = refs[9:]
            views += [cols(ys_ref, gw, u), cols(y_ref, gw, u)]
        _ssd_group(*views, carry=carry, hpg=hpg, hd=hd, **kw)


def _ssd_group(*refs, seg, lo, hi, tile, carry, hpg, hd, nchunk):
    if carry:
        (z_ref, x_ref, b_ref, c_ref, dtT_ref, acol_ref, aw_ref, dw_ref, gw_ref,
         y_ref, st_ref, state) = refs
    else:
        (z_ref, x_ref, b_ref, c_ref, dtT_ref, acol_ref, aw_ref, dw_ref, gw_ref,
         ys_ref, y_ref) = refs
    ci = pl.program_id(2)
    gw = hpg * hd
    x = x_ref[0]
    Bm = b_ref[0]
    Cm = c_ref[0]

    lane = lax.broadcasted_iota(jnp.int32, (1, CHUNK), 1)
    pos = ci * CHUNK + lane
    if tile is not None:
        pos = pos & (tile - 1)
    valid = (pos >= lo) & (pos < hi)
    dtT = jnp.where(valid, dtT_ref[...], 0.0)

    ii = lax.broadcasted_iota(jnp.int32, (CHUNK, CHUNK), 0)
    jj = lax.broadcasted_iota(jnp.int32, (CHUNK, CHUNK), 1)
    causal = jj <= ii
    upper = ii <= jj
    if seg < CHUNK:
        sh = int(math.log2(seg))
        same = (ii >> sh) == (jj >> sh)
        causal = causal & same
        upper = upper & same
    Lc = jnp.where(causal, 1.0, 0.0).astype(BF16)
    LT = jnp.where(upper, 1.0, 0.0).astype(BF16)
    eye = jnp.where(ii == jj, 1.0, 0.0).astype(BF16)

    parts = _split3(dtT)
    cum_row = sum(_dot(p, LT) for p in parts) * acol_ref[...]

    def widen(a):
        return jnp.broadcast_to(a[:, None, :], (hpg, hd, CHUNK)).reshape(gw, CHUNK)

    LI = jnp.concatenate([Lc, eye], axis=0)
    G = sum(_dot_nt(LI, widen(p.astype(F32)).astype(BF16)) for p in parts)
    cumcol = G[:CHUNK] * aw_ref[...]
    dtcol = G[CHUNK:]

    cb = _dot_nt(Cm.astype(BF16), Bm.astype(BF16))
    per = LANES // hd
    lanep = lax.broadcasted_iota(jnp.int32, (1, LANES), 1)
    ys = []
    for sp in range(gw // LANES):
        xp = x[:, sp * LANES:(sp + 1) * LANES]
        Ms, Xs = [], []
        for hh in range(per):
            h = sp * per + hh
            ccol = cumcol[:, h * hd:h * hd + 1]
            crow = cum_row[h:h + 1, :]
            dec = jnp.exp(jnp.where(causal, ccol - crow, NEG))
            Ms.append((dec * cb * dtT[h:h + 1, :]).astype(BF16))
            sel = (lanep >= hh * hd) & (lanep < (hh + 1) * hd)
            Xs.append(jnp.where(sel, xp, 0.0).astype(BF16))
        ys.append(_dot(jnp.concatenate(Ms, axis=1), jnp.concatenate(Xs, axis=0)))
    y = jnp.concatenate(ys, axis=1)

    if carry:
        @pl.when(ci == 0)
        def _():
            state[...] = jnp.zeros_like(state)

        ST = state[...]
        yst = _dot(Cm.astype(BF16), ST.astype(BF16))
    else:
        yst = ys_ref[0]
    y = y + yst * jnp.exp(cumcol)

    if carry:
        last = cumcol[CHUNK - 1:CHUNK, :]
        xw = (x * (jnp.exp(last - cumcol) * dtcol)).astype(BF16)
        new = ST * jnp.exp(last) + _dot(Bm.T.astype(BF16), xw)
        state[...] = new
        st_ref[0, 0] = new

    y = y + x * dw_ref[...]
    y = y * _silu(z_ref[0])
    ms = jnp.mean(y * y, axis=-1, keepdims=True)
    y_ref[0] = (y * lax.rsqrt(ms + EPS) * gw_ref[...]).astype(BF16)


def ssd_chunks(z3, act3, dtT, A, D, gate_norm, dims, *, seg, lo, hi, tile, ystate=None, name):
    H, hd, G, N = dims
    hpg = H // G
    gw = hpg * hd
    d_inner = H * hd
    Bt, T, _ = act3.shape
    nchunk = T // CHUNK
    carry = ystate is None
    gpb = 8 if G % 8 == 0 and (d_inner // N) % 8 == 0 else 1
    boff = d_inner // (N * gpb)
    coff = (d_inner + G * N) // (N * gpb)
    sh, sw, sn, ng = hpg * gpb, gw * gpb, N * gpb, G // gpb
    in_specs = [
        pl.BlockSpec((1, CHUNK, sw), lambda b, g, c: (b, c, g)),
        pl.BlockSpec((1, CHUNK, sw), lambda b, g, c: (b, c, g)),
        pl.BlockSpec((1, CHUNK, sn), lambda b, g, c: (b, c, boff + g)),
        pl.BlockSpec((1, CHUNK, sn), lambda b, g, c: (b, c, coff + g)),
        pl.BlockSpec((sh, CHUNK), lambda b, g, c: (g, b * nchunk + c)),
        pl.BlockSpec((sh, 1), lambda b, g, c: (g, 0)),
        pl.BlockSpec((1, sw), lambda b, g, c: (0, g)),
        pl.BlockSpec((1, sw), lambda b, g, c: (0, g)),
        pl.BlockSpec((1, sw), lambda b, g, c: (0, g)),
    ]
    args = [z3, act3, act3, act3, dtT, A.reshape(H, 1),
            jnp.repeat(A, hd).reshape(1, d_inner),
            jnp.repeat(D.astype(F32), hd).reshape(1, d_inner),
            gate_norm.astype(F32).reshape(1, d_inner)]
    y_spec = pl.BlockSpec((1, CHUNK, sw), lambda b, g, c: (b, c, g))
    y_shape = jax.ShapeDtypeStruct((Bt, T, d_inner), BF16)
    kern = functools.partial(_ssd_kernel, gpb=gpb, n_state=N, seg=seg, lo=lo, hi=hi, tile=tile, carry=carry,
                             hpg=hpg, hd=hd, nchunk=nchunk)
    if carry:
        return pl.pallas_call(
            kern, grid=(Bt, ng, nchunk), in_specs=in_specs,
            out_specs=[y_spec, pl.BlockSpec((1, gpb, N, gw), lambda b, g, c: (b, g, 0, 0))],
            out_shape=[y_shape, jax.ShapeDtypeStruct((Bt, G, N, gw), F32)],
            scratch_shapes=[pltpu.VMEM((gpb, N, gw), F32)],
            compiler_params=_cp("parallel", "parallel", "arbitrary"),
            name=name,
        )(*args)
    in_specs.append(pl.BlockSpec((1, CHUNK, sw), lambda b, g, c: (b, c, g)))
    args.append(ystate)
    return pl.pallas_call(
        kern, grid=(Bt, ng, nchunk), in_specs=in_specs, out_specs=y_spec, out_shape=y_shape,
        compiler_params=_cp("parallel", "parallel", "arbitrary"),
        name=name,
    )(*args)


def _sstate_kernel(*refs, H, hd, G, N, tile, lo, chained):
    if chained:
        s0_ref, act_ref, dtT_ref, acol_ref, _, ys_ref, sn_ref, xwT = refs
    else:
        s0_ref, act_ref, dtT_ref, acol_ref, ys_ref, sn_ref, xwT = refs
    d_inner = H * hd
    gw = d_inner // G
    per_blk = CHUNK // tile
    sh = int(math.log2(tile))
    sub = pl.program_id(0) % per_blk
    r0 = pl.multiple_of(sub * tile, tile)
    lane = lax.broadcasted_iota(jnp.int32, (1, CHUNK), 1)
    valid = (lane & (tile - 1)) >= lo
    dtT = jnp.where(valid, dtT_ref[...], 0.0)
    acol = acol_ref[...]

    @pl.when(sub == 0)
    def _():
        ii = lax.broadcasted_iota(jnp.int32, (CHUNK, CHUNK), 0)
        jj = lax.broadcasted_iota(jnp.int32, (CHUNK, CHUNK), 1)
        U = jnp.where((ii > jj) & ((ii >> sh) == (jj >> sh)), 1.0, 0.0).astype(BF16)
        suf = sum(_dot(p, U) for p in _split3(dtT)) * acol
        wd = jnp.exp(suf) * dtT
        wdw = jnp.broadcast_to(wd[:, None, :], (H, hd, CHUNK)).reshape(d_inner, CHUNK)
        xT = act_ref[:, 0:d_inner].T
        xwT[...] = (xT * wdw).astype(BF16)

    own = (lane >> sh) == sub
    last = jnp.sum(jnp.where(own, dtT, 0.0), axis=1, keepdims=True) * acol
    dec = jnp.exp(last)
    dec_col = jnp.broadcast_to(dec[:, None, :], (H, hd, 1)).reshape(d_inner, 1)
    rown = (lax.broadcasted_iota(jnp.int32, (CHUNK, 1), 0) >> sh) == sub
    for g in range(G):
        S0 = s0_ref[0, 0, g * gw:(g + 1) * gw, :]
        Cg = act_ref[pl.ds(r0, tile), d_inner + G * N + g * N:d_inner + G * N + (g + 1) * N]
        ys_ref[pl.ds(r0, tile), g * gw:(g + 1) * gw] = _dot_nt(Cg, S0)
        Bg = jnp.where(rown, act_ref[:, d_inner + g * N:d_inner + (g + 1) * N], 0.0).astype(BF16)
        dS = _dot(xwT[g * gw:(g + 1) * gw, :], Bg)
        sn_ref[0, 0, g * gw:(g + 1) * gw, :] = S0 * dec_col[g * gw:(g + 1) * gw] + dS


def sample_state(s_all, layer, act2, dtT, A, dims, *, tile, lo, prev, name):
    H, hd, G, N = dims
    d_inner = H * hd
    nseq = s_all.shape[1]
    rows, cdim = act2.shape
    per_blk = CHUNK // tile
    chained = prev is not None
    in_specs = [pl.BlockSpec((1, 1, d_inner, N), lambda b: (layer, b, 0, 0)),
                pl.BlockSpec((CHUNK, cdim), lambda b: (b // per_blk, 0)),
                pl.BlockSpec((H, CHUNK), lambda b: (0, b // per_blk)),
                pl.BlockSpec((H, 1), lambda b: (0, 0))]
    args = [s_all, act2, dtT, A.reshape(H, 1)]
    if chained:
        in_specs.append(pl.BlockSpec(memory_space=pl.ANY))
        args.append(prev)
    return pl.pallas_call(
        functools.partial(_sstate_kernel, H=H, hd=hd, G=G, N=N, tile=tile, lo=lo, chained=chained),
        grid=(nseq,),
        in_specs=in_specs,
        out_specs=[pl.BlockSpec((CHUNK, d_inner), lambda b: (b // per_blk, 0)),
                   pl.BlockSpec((1, 1, d_inner, N), lambda b: (layer, b, 0, 0))],
        out_shape=[jax.ShapeDtypeStruct((rows, d_inner), F32),
                   jax.ShapeDtypeStruct(s_all.shape, F32)],
        scratch_shapes=[pltpu.VMEM((d_inner, CHUNK), BF16)],
        input_output_aliases={4: 1} if chained else {},
        compiler_params=_cp("arbitrary"),
        name=name,
    )(*args)


def _kvpost_kernel(a_ref, g_ref, cs_ref, rows_ref, kb_ref, *, lora, rope):
    a = a_ref[...]
    c = a[:, :lora]
    ms = jnp.mean(c * c, axis=-1, keepdims=True)
    cn = c * lax.rsqrt(ms + EPS) * g_ref[...]
    u = a[:, lora:lora + LANES] * cs_ref[...]
    kr = u + pltpu.roll(u, rope, 1)
    lane = lax.broadcasted_iota(jnp.int32, (1, LANES), 1)
    rows_ref[:, :lora] = cn
    rows_ref[:, lora:lora + rope] = kr[:, :rope]
    kb_ref[:, :lora] = cn.astype(BF16)
    kb_ref[:, lora:lora + LANES] = jnp.where(lane < rope, kr, 0.0).astype(BF16)


def kv_post(a, g, cs, lora, rope, tm, name):
    M = a.shape[0]
    assert 2 * rope == LANES
    wa = lora + LANES
    return pl.pallas_call(
        functools.partial(_kvpost_kernel, lora=lora, rope=rope),
        grid=(M // tm,),
        in_specs=[pl.BlockSpec((tm, wa), lambda i: (i, 0)),
                  pl.BlockSpec((1, lora), lambda i: (0, 0)),
                  pl.BlockSpec((tm, LANES), lambda i: (i, 0))],
        out_specs=[pl.BlockSpec((tm, lora + rope), lambda i: (i, 0)),
                   pl.BlockSpec((tm, wa), lambda i: (i, 0))],
        out_shape=[jax.ShapeDtypeStruct((M, lora + rope), F32),
                   jax.ShapeDtypeStruct((M, wa), BF16)],
        compiler_params=_cp("parallel"),
        name=name,
    )(a, g.reshape(1, lora), cs)


def _key_inv_rms(kb, waug, heads, nope, qk):
    n = kb.shape[0]
    kn = _dot_nt(waug, kb)
    sq = kn * kn
    ssh = jnp.sum(sq[:heads * nope].reshape(heads, nope, n), axis=1)
    ssr = jnp.sum(sq[heads * nope:], axis=0, keepdims=True)
    return lax.rsqrt((ssh + ssr) / qk + EPS)


def _rms_kernel(kb_ref, w_ref, o_ref, *, heads, nope, qk):
    o_ref[...] = _key_inv_rms(kb_ref[...], w_ref[...], heads, nope, qk)


def key_rms(kb, waug, heads, nope, qk, tk, name):
    M, W = kb.shape
    R = waug.shape[0]
    return pl.pallas_call(
        functools.partial(_rms_kernel, heads=heads, nope=nope, qk=qk),
        grid=(M // tk,),
        in_specs=[pl.BlockSpec((tk, W), lambda i: (i, 0)),
                  pl.BlockSpec((R, W), lambda i: (0, 0))],
        out_specs=pl.BlockSpec((heads, tk), lambda i: (0, i)),
        out_shape=jax.ShapeDtypeStruct((heads, M), F32),
        compiler_params=_cp("parallel"),
        name=name,
    )(kb, waug)


def _q_kernel(p_ref, g_ref, wq_ref, cs_ref, qn_ref, kn_ref, wuk_ref, o_ref, *, heads, nope, rope, lora, qk):
    p = p_ref[...]
    ms = jnp.mean(p * p, axis=-1, keepdims=True)
    qa = (p * lax.rsqrt(ms + EPS) * g_ref[...]).astype(BF16)
    lane = lax.broadcasted_iota(jnp.int32, (1, LANES), 1)
    cs = cs_ref[...]
    sc = qn_ref[...] * kn_ref[...] * (1.0 / math.sqrt(qk))
    hw = nope + LANES
    qall = _dot(qa, wq_ref[...])
    qns = []
    for h in range(heads):
        qnope = qall[:, h * hw:h * hw + nope]
        u = qall[:, h * hw + nope:(h + 1) * hw] * cs
        qr = u + pltpu.roll(u, rope, 1)
        qr = jnp.where(lane < rope, qr, 0.0)
        ss = jnp.sum(qnope * qnope, axis=-1, keepdims=True) + jnp.sum(qr * qr, axis=-1, keepdims=True)
        inv = lax.rsqrt(ss / qk + EPS)
        qns.append((qnope * inv * sc[:, :nope]).astype(BF16))
        o_ref[h, :, lora:lora + LANES] = (qr * inv * sc[:, nope:]).astype(BF16)
    for h in range(heads):
        o_ref[h, :, :lora] = _dot(qns[h], wuk_ref[h]).astype(BF16)


def q_side(proj, g, wq_ext, cs, qn_ext, kn_ext, wukT, dims, tm, name):
    heads, nope, rope, lora, qk = dims
    M = proj.shape[0]
    qlora = g.shape[0]
    hw = nope + LANES
    return pl.pallas_call(
        functools.partial(_q_kernel, heads=heads, nope=nope, rope=rope, lora=lora, qk=qk),
        grid=(M // tm,),
        in_specs=[pl.BlockSpec((tm, qlora), lambda i: (i, 0)),
                  pl.BlockSpec((1, qlora), lambda i: (0, 0)),
                  pl.BlockSpec((qlora, heads * hw), lambda i: (0, 0)),
                  pl.BlockSpec((tm, LANES), lambda i: (i, 0)),
                  pl.BlockSpec((1, hw), lambda i: (0, 0)),
                  pl.BlockSpec((1, hw), lambda i: (0, 0)),
                  pl.BlockSpec((heads, nope, lora), lambda i: (0, 0, 0))],
        out_specs=pl.BlockSpec((heads, tm, lora + LANES), lambda i: (0, i, 0)),
        out_shape=jax.ShapeDtypeStruct((heads, M, lora + LANES), BF16),
        compiler_params=_cp("parallel"),
        name=name,
    )(proj, g.reshape(1, qlora), wq_ext, cs, qn_ext, kn_ext, wukT)


def _softmax_update(s, v, m_sc, l_sc, acc_sc, rows=None, v_transposed=False):
    rows = slice(None) if rows is None else rows
    m_old = m_sc[rows]
    m_new = jnp.maximum(m_old, jnp.max(s, axis=-1, keepdims=True))
    alpha = jnp.exp(m_old - m_new)
    p = jnp.exp(s - m_new)
    l_sc[rows] = alpha * l_sc[rows] + jnp.sum(p, axis=-1, keepdims=True)
    pv = _dot_nt(p.astype(BF16), v) if v_transposed else _dot(p.astype(BF16), v)
    acc_sc[rows] = alpha * acc_sc[rows] + pv
    m_sc[rows] = m_new


def _attn_init(m_sc, l_sc, acc_sc):
    m_sc[...] = jnp.full_like(m_sc, NEG)
    l_sc[...] = jnp.zeros_like(l_sc)
    acc_sc[...] = jnp.zeros_like(acc_sc)


def _attn_finish(o_ref, wuv_ref, l_sc, acc_sc, heads, tq, vd, head0=0):
    inv = 1.0 / l_sc[...]
    for h in range(heads):
        oh = (acc_sc[h * tq:(h + 1) * tq, :] * inv[h * tq:(h + 1) * tq]).astype(BF16)
        o_ref[0, :, (head0 + h) * vd:(head0 + h + 1) * vd] = _dot(oh, wuv_ref[head0 + h])


def _pattn_kernel(q_ref, k_ref, r_ref, wuv_ref, o_ref, *scratch, hg, nslab, tq, ck, lora, vd):
    qi = pl.program_id(1)
    gi = pl.program_id(2)
    W = q_ref.shape[-1]
    hs = hg // nslab
    slabs = [(scratch[u], scratch[nslab + u], scratch[2 * nslab + u]) for u in range(nslab)]
    for st in slabs:
        _attn_init(*st)

    def chunk(c, masked):
        k0 = pl.multiple_of(c * ck, ck)
        Kc = k_ref[0, pl.ds(k0, ck), :]
        if masked:
            ti = qi * tq + lax.broadcasted_iota(jnp.int32, (1, tq, ck), 1)
            kj = k0 + lax.broadcasted_iota(jnp.int32, (1, tq, ck), 2)
            keep = kj <= ti
        rg = r_ref[0, c, pl.ds(pl.multiple_of(gi * hg, hg), hg), :]
        for u, st in enumerate(slabs):
            Q = q_ref[u * hs:(u + 1) * hs, 0].reshape(hs * tq, W)
            rr = rg[u * hs:(u + 1) * hs]
            s = _dot_nt(Q, Kc).reshape(hs, tq, ck) * rr[:, None, :]
            if masked:
                s = jnp.where(keep, s, NEG)
            _softmax_update(s.reshape(hs * tq, ck), Kc[:, :lora], *st)

    nfull = (qi * tq) // ck

    def body(c, carry):
        chunk(c, False)
        return carry

    lax.fori_loop(0, nfull, body, 0)
    chunk(nfull, True)
    for u, (_, l_sc, acc_sc) in enumerate(slabs):
        _attn_finish(o_ref, wuv_ref, l_sc, acc_sc, hs, tq, vd, head0=u * hs)


def prompt_attention(q4, kb3, r4, wuv, tq, ck, hg, nslab, name):
    heads, B, T, W = q4.shape
    Tk = kb3.shape[1]
    lora, vd = wuv.shape[1], wuv.shape[2]
    rows = hg // nslab * tq
    return pl.pallas_call(
        functools.partial(_pattn_kernel, hg=hg, nslab=nslab, tq=tq, ck=ck, lora=lora, vd=vd),
        grid=(B, T // tq, heads // hg),
        in_specs=[pl.BlockSpec((hg, 1, tq, W), lambda b, i, g: (g, b, i, 0)),
                  pl.BlockSpec((1, Tk, W), lambda b, i, g: (b, 0, 0)),
                  pl.BlockSpec((1, Tk // ck, heads, ck), lambda b, i, g: (b, 0, 0, 0)),
                  pl.BlockSpec((hg, lora, vd), lambda b, i, g: (g, 0, 0))],
        out_specs=pl.BlockSpec((1, tq, hg * vd), lambda b, i, g: (b, i, g)),
        out_shape=jax.ShapeDtypeStruct((B, T, heads * vd), F32),
        scratch_shapes=([pltpu.VMEM((rows, 1), F32)] * (2 * nslab) + [pltpu.VMEM((rows, lora), F32)] * nslab),
        compiler_params=_cp("parallel", "parallel", "arbitrary"),
        name=name,
    )(q4, kb3, r4, wuv)


def _sattn_kernel(pt_ref, q_ref, *refs, npp, nsteps, heads, tq, page, lora, vd, nope, qk, group, make_r):
    pages = refs[:npp]
    if make_r:
        wuk_ref, kn_ref, rn_ref, wuv_ref, o_ref, rp_ref, m_sc, l_sc, acc_sc, kt_sc = refs[npp:]
    else:
        rp_ref, kn_ref, rn_ref, wuv_ref, o_ref, m_sc, l_sc, acc_sc, kt_sc = refs[npp:]
    step = pl.program_id(1)

    @pl.when(step == 0)
    def _():
        _attn_init(m_sc, l_sc, acc_sc)

    Q = q_ref[0]
    ssr = []
    for i in range(npp):
        pg = pages[i][0]
        kt_sc[:, i * page:(i + 1) * page] = pg.astype(BF16)
        if make_r:
            kr = pg[lora:]
            ssr.append(jnp.sum(kr * kr, axis=0, keepdims=True))
    nk = npp * page
    if make_r:
        for g in range(0, npp, group):
            n0, n1 = g * page, (g + group) * page
            kn = _dot(wuk_ref[...], kt_sc[:lora, n0:n1])
            ssh = jnp.sum((kn * kn).reshape(heads, nope, n1 - n0), axis=1)
            rp_ref[0, :, n0:n1] = lax.rsqrt((ssh + jnp.concatenate(ssr[g:g + group], axis=1)) / qk + EPS)
    s = _dot(Q, kt_sc[...]).reshape(heads, tq, nk) * rp_ref[0][:, None, :]
    _softmax_update(s.reshape(heads * tq, nk), kt_sc[:lora, :], m_sc, l_sc, acc_sc, v_transposed=True)

    @pl.when(step == nsteps - 1)
    def _():
        Kn = kn_ref[0]
        s = _dot(Q, Kn).reshape(heads, tq, page) * rn_ref[0][:, None, :]
        ti = lax.broadcasted_iota(jnp.int32, (1, tq, page), 1)
        kj = lax.broadcasted_iota(jnp.int32, (1, tq, page), 2)
        s = jnp.where(kj <= ti, s, NEG)
        _softmax_update(s.reshape(heads * tq, page), Kn[:lora, :], m_sc, l_sc, acc_sc, v_transposed=True)
        _attn_finish(o_ref, wuv_ref, l_sc, acc_sc, heads, tq, vd)


def sample_attention(page_table, q3, cacheT, r_or_wuk, knewT, rnew, wuv, tq, npp, nope, qk, make_r, name):
    nseq, npages = page_table.shape
    _, W, page = cacheT.shape
    heads, lora, vd = wuv.shape
    nsteps = npages // npp
    group = 4 if npp % 4 == 0 else 1
    page_specs = [pl.BlockSpec((1, W, page), (lambda b, s, pt, i=i: (pt[b, s * npp + i], 0, 0)))
                  for i in range(npp)]
    r_spec = pl.BlockSpec((1, heads, npp * page), lambda b, s, pt: (b, 0, s))
    o_spec = pl.BlockSpec((1, tq, heads * vd), lambda b, s, pt: (b, 0, 0))
    o_shape = jax.ShapeDtypeStruct((nseq, tq, heads * vd), F32)
    if make_r:
        first = pl.BlockSpec(r_or_wuk.shape, lambda b, s, pt: (0, 0))
        out_specs = [o_spec, r_spec]
        out_shape = [o_shape, jax.ShapeDtypeStruct((nseq, heads, npages * page), F32)]
    else:
        first = r_spec
        out_specs, out_shape = o_spec, o_shape
    return pl.pallas_call(
        functools.partial(_sattn_kernel, npp=npp, nsteps=nsteps, heads=heads, tq=tq, page=page,
                          lora=lora, vd=vd, nope=nope, qk=qk, group=group, make_r=make_r),
        grid_spec=pltpu.PrefetchScalarGridSpec(
            num_scalar_prefetch=1,
            grid=(nseq, nsteps),
            in_specs=[pl.BlockSpec((1, heads * tq, W), lambda b, s, pt: (b, 0, 0))] + page_specs + [
                first,
                pl.BlockSpec((1, W, page), lambda b, s, pt: (b, 0, 0)),
                pl.BlockSpec((1, heads, page), lambda b, s, pt: (b, 0, 0)),
                pl.BlockSpec((heads, lora, vd), lambda b, s, pt: (0, 0, 0))],
            out_specs=out_specs,
            scratch_shapes=[pltpu.VMEM((heads * tq, 1), F32), pltpu.VMEM((heads * tq, 1), F32),
                            pltpu.VMEM((heads * tq, lora), F32), pltpu.VMEM((W, npp * page), BF16)],
        ),
        out_shape=out_shape,
        compiler_params=_cp("parallel", "arbitrary"),
        name=name,
    )(page_table, q3, *([cacheT] * npp), r_or_wuk, knewT, rnew, wuv)


def _rope_table(pos, rope):
    inv = 1.0 / (ROPE_BASE ** (jnp.arange(0, rope, 2, dtype=F32) / rope))
    f = pos.astype(F32)[:, None] * inv[None, :]
    emb = jnp.concatenate([f, f], -1)
    return jnp.concatenate([jnp.cos(emb), jnp.sin(emb)], -1)


def _rot_cols(w, rope):
    h = rope // 2
    return jnp.concatenate([-w[..., h:], w[..., :h]], -1)


def kernel(x_prompt, x_sample, state_ssm, state_conv, cache_kv, page_table, meta_tokens,
           a_norm, a_w_in, a_conv_w, a_conv_b, a_dt_bias, a_A_log, a_D, a_gate_norm, a_w_out,
           kv_norm, w_kv_a, kv_a_norm, w_uk, w_uv, k_norm,
           b_norm, b_w_in, b_q_a_norm, b_w_q, b_q_norm, b_w_out):
    bt, L, dm = x_prompt.shape
    nseq, S, _ = x_sample.shape
    n_meta = meta_tokens.shape[0]
    n_a = a_w_in.shape[0]
    n_b = b_w_in.shape[0]
    _, _, H, hd, N = state_ssm.shape
    cw, cdim = a_conv_w.shape[1:]
    d_inner = a_w_out.shape[1]
    G = (cdim - d_inner) // (2 * N)
    sdims = (H, hd, G, N)
    lora, heads, nope = w_uk.shape
    vd = w_uv.shape[2]
    kvrow = cache_kv.shape[2]
    rope = kvrow - lora
    qk = nope + rope
    qlora = b_q_a_norm.shape[1]
    page = cache_kv.shape[1]
    past = page_table.shape[1] * page
    tile = SUBLANES
    assert cw - 1 + S + 1 == tile and CHUNK % tile == 0 and (nseq * tile) % CHUNK == 0

    T = L + n_meta
    Tp = -(-T // CHUNK) * CHUNK
    Mp = bt * Tp
    Ms = nseq * S
    tmp = _pick(Mp, 512)
    tms = _pick(Ms, 512)

    xp = jnp.concatenate([jnp.broadcast_to(meta_tokens[None], (bt, n_meta, dm)), x_prompt,
                          jnp.zeros((bt, Tp - T, dm), F32)], 1).reshape(Mp, dm)
    xs = x_sample.reshape(Ms, dm)

    ssm_p, conv_p, conv_s = [], [], []
    s_new = None
    a_w_inT = jnp.transpose(a_w_in, (0, 2, 1))
    for i in range(n_a):
        w_out = a_w_out[i].astype(BF16)
        A = -jnp.exp(a_A_log[i].astype(F32))
        tn = _pick(d_inner + cdim, 512)
        tmw = _pick(Mp, 1088)

        proj = norm_matmul_t(xp, a_norm[i], a_w_inT, i, d_inner + cdim, tmw, tn, name=f"a{i}_in_p")
        dt = dt_proj(xp, a_norm[i], a_w_inT, i, d_inner + cdim, a_dt_bias[i], tmp, name=f"a{i}_dt_p")
        proj3 = proj.reshape(bt, Tp, d_inner + cdim)
        act = conv_silu(proj3, d_inner, a_conv_w[i], a_conv_b[i], 256,True, name=f"a{i}_conv_p")
        y, st = ssd_chunks(proj3, act, dt.T, A, a_D[i], a_gate_norm[i], sdims,
                           seg=CHUNK, lo=0, hi=T, tile=None, name=f"a{i}_ssd_p")
        xp = matmul_res(y.reshape(Mp, d_inner), w_out, xp, tmw, 512, d_inner, name=f"a{i}_out_p")
        conv_p.append(proj3[:, T - (cw - 1):T, d_inner:])
        ssm_p.append(st.reshape(bt, G, N, H // G, hd).transpose(0, 1, 3, 4, 2).reshape(bt, H, hd, N))

        proj_s = norm_matmul_t(xs, a_norm[i], a_w_inT, i, d_inner + cdim, tms, tn, name=f"a{i}_in_s")
        dt_s = dt_proj(xs, a_norm[i], a_w_inT, i, d_inner + cdim, a_dt_bias[i], tms, name=f"a{i}_dt_s")
        ps3 = proj_s.reshape(nseq, S, d_inner + cdim)
        z8 = jnp.concatenate([jnp.zeros((nseq, tile - S, d_inner), F32), ps3[..., :d_inner]], 1)
        full8 = jnp.concatenate([jnp.zeros((nseq, 1, cdim), F32), state_conv[i], ps3[..., d_inner:]], 1)
        conv_s.append(full8[:, tile - (cw - 1):])
        dt8T = jnp.concatenate([jnp.zeros((nseq, tile - S, H), F32), dt_s.reshape(nseq, S, H)],
                               1).reshape(nseq * tile, H).T
        act8 = conv_silu(full8.reshape(1, nseq * tile, cdim), 0, a_conv_w[i], a_conv_b[i], 256,False,
                         name=f"a{i}_conv_s")
        ys, s_new = sample_state(state_ssm.reshape(n_a, nseq, d_inner, N), i, act8[0], dt8T, A, sdims,
                                 tile=tile, lo=tile - S, prev=s_new, name=f"a{i}_state_s")
        nblk = nseq * tile // CHUNK
        y8 = ssd_chunks(z8.reshape(nblk, CHUNK, d_inner), act8.reshape(nblk, CHUNK, cdim), dt8T, A, a_D[i],
                        a_gate_norm[i], sdims, seg=tile, lo=tile - S, hi=tile, tile=tile,
                        ystate=ys.reshape(nblk, CHUNK, d_inner), name=f"a{i}_ssd_s")
        y_s = y8.reshape(nseq, tile, d_inner)[:, tile - S:].reshape(Ms, d_inner)
        xs = matmul_res(y_s, w_out, xs, tms, dm, 512, name=f"a{i}_out_s")

    wa = lora + LANES
    w_kv_ext = jnp.concatenate([w_kv_a, _rot_cols(w_kv_a[:, lora:], rope)], 1).astype(BF16)
    cs_p = jnp.tile(_rope_table(jnp.arange(Tp), rope), (bt, 1))
    cs_s = jnp.tile(_rope_table(past + jnp.arange(S), rope), (nseq, 1))
    wukT2 = w_uk.transpose(1, 2, 0).reshape(heads * nope, lora)
    waug = jnp.zeros((heads * nope + rope, wa), F32)
    waug = waug.at[:heads * nope, :lora].set(wukT2)
    waug = waug.at[heads * nope:, lora:lora + rope].set(jnp.eye(rope, dtype=F32)).astype(BF16)
    wukT = w_uk.transpose(1, 2, 0).astype(BF16)
    wuv = w_uv.transpose(1, 0, 2).astype(BF16)

    a_p = norm_matmul(xp, kv_norm, w_kv_ext, tmp, wa, name="kv_a_p")
    rows_p, kb_p = kv_post(a_p, kv_a_norm, cs_p, lora, rope, tmp, name="kv_post_p")
    rT_p = key_rms(kb_p, waug, heads, nope, qk, _pick(Mp, 256), name="kv_rms_p")
    a_s = norm_matmul(xs, kv_norm, w_kv_ext, tms, wa, name="kv_a_s")
    rows_s, kb_s = kv_post(a_s, kv_a_norm, cs_s, lora, rope, tms, name="kv_post_s")
    rT_s = key_rms(kb_s, waug, heads, nope, qk, _pick(Ms, 256), name="kv_rms_s")
    npp = min(32, page_table.shape[1])
    cacheT = jnp.transpose(cache_kv, (0, 2, 1))
    r_past = None

    ck = 4 * CHUNK
    Tk = -(-Tp // ck) * ck
    kb3 = jnp.pad(kb_p.reshape(bt, Tp, wa), ((0, 0), (0, Tk - Tp), (0, 0)))
    r4 = jnp.pad(rT_p.reshape(heads, bt, Tp), ((0, 0), (0, 0), (0, Tk - Tp)), constant_values=1.0)
    r4 = r4.reshape(heads, bt, Tk // ck, ck).transpose(1, 2, 0, 3)
    knew = jnp.pad(kb_s[:, :kvrow].reshape(nseq, S, kvrow), ((0, 0), (0, page - S), (0, 0))).transpose(0, 2, 1)
    rnew = jnp.pad(rT_s.reshape(heads, nseq, S).transpose(1, 0, 2), ((0, 0), (0, 0), (0, page - S)),
                   constant_values=1.0)

    def ext(v):
        return jnp.pad(v.astype(F32), (0, LANES - rope)).reshape(1, nope + LANES)

    qdims = (heads, nope, rope, lora, qk)
    tq = CHUNK
    for j in range(n_b):
        w_in = b_w_in[j].astype(BF16)
        wq3 = b_w_q[j].reshape(qlora, heads, qk)
        wq_ext = jnp.concatenate([wq3, _rot_cols(wq3[..., nope:], rope)], -1).reshape(qlora, heads * (nope + LANES))
        wq_ext = wq_ext.astype(BF16)
        w_out = b_w_out[j].astype(BF16)
        tn = _pick(w_in.shape[1], 1280)

        proj = norm_matmul(xp, b_norm[j], w_in, tmp, tn, name=f"b{j}_in_p")
        q = q_side(proj, b_q_a_norm[j], wq_ext, cs_p, ext(b_q_norm[j]), ext(k_norm), wukT, qdims,
                   _pick(Mp, 256), name=f"b{j}_q_p")
        o = prompt_attention(q.reshape(heads, bt, Tp, wa), kb3, r4, wuv, tq, ck, min(16, heads), 1,
                             name=f"b{j}_attn_p")
        xp = matmul_res(o.reshape(Mp, heads * vd), w_out, xp, tmp, dm, 512, name=f"b{j}_out_p",
                        gate=proj, gate_col0=qlora)

        proj_s = norm_matmul(xs, b_norm[j], w_in, tms, tn, name=f"b{j}_in_s")
        q_s = q_side(proj_s, b_q_a_norm[j], wq_ext, cs_s, ext(b_q_norm[j]), ext(k_norm), wukT, qdims,
                     _pick(Ms, 256), name=f"b{j}_q_s")
        q8 = jnp.pad(q_s[..., :kvrow].reshape(heads, nseq, S, kvrow), ((0, 0), (0, 0), (0, tile - S), (0, 0)))
        q8 = q8.transpose(1, 0, 2, 3).reshape(nseq, heads * tile, kvrow)
        if r_past is None:
            o_s, r_past = sample_attention(page_table, q8, cacheT, wukT2.astype(BF16), knew, rnew, wuv, tile, npp,
                                           nope, qk, True, name=f"b{j}_attn_s")
        else:
            o_s = sample_attention(page_table, q8, cacheT, r_past, knew, rnew, wuv, tile, npp,
                                   nope, qk, False, name=f"b{j}_attn_s")
        xs = matmul_res(o_s[:, :S].reshape(Ms, heads * vd), w_out, xs, tms, dm, 512, name=f"b{j}_out_s",
                        gate=proj_s, gate_col0=qlora)

    y_prompt = xp.reshape(bt, Tp, dm)[:, n_meta:T]
    kv_p = rows_p.reshape(bt, Tp, kvrow)[:, :T]
    return (y_prompt, xs.reshape(nseq, S, dm), jnp.stack(ssm_p), jnp.stack(conv_p), kv_p,
            s_new.reshape(n_a, nseq, H, hd, N), jnp.stack(conv_s), rows_s.reshape(nseq, S, kvrow))
```

```python
import functools
import math

import jax
import jax.numpy as jnp
from jax import lax
from jax.experimental import pallas as pl
from jax.experimental.pallas import tpu as pltpu

F32 = jnp.float32
BF16 = jnp.bfloat16
EPS = 1e-6
ROPE_BASE = 10000.0
LANES = 128
SUBLANES = 8
CHUNK = 128
NEG = -1e30
VMEM_LIMIT = 48 * 1024 * 1024


def _cp(*sem):
    return pltpu.CompilerParams(dimension_semantics=sem, vmem_limit_bytes=VMEM_LIMIT)


def _dot(a, b):
    return jnp.dot(a, b, preferred_element_type=F32)


def _dot_nt(a, b):
    return lax.dot_general(a, b, (((1,), (1,)), ((), ())), preferred_element_type=F32)


def _split3(a):
    h = a.astype(BF16)
    r = a - h.astype(F32)
    m = r.astype(BF16)
    l = (r - m.astype(F32)).astype(BF16)
    return h, m, l


def _silu(v):
    return v * jax.nn.sigmoid(v)


def _pick(n, target):
    best = None
    for t in range(8, min(n, target) + 1, 8):
        if n % t == 0:
            best = t
    assert best is not None, (n, target)
    return best


def _norm_matmul_kernel(x_ref, g_ref, w_ref, o_ref, xn_ref):
    @pl.when(pl.program_id(1) == 0)
    def _():
        x = x_ref[...]
        ms = jnp.mean(x * x, axis=-1, keepdims=True)
        xn_ref[...] = (x * lax.rsqrt(ms + EPS) * g_ref[...]).astype(BF16)

    o_ref[...] = _dot(xn_ref[...], w_ref[...])


def norm_matmul(x, g, w, tm, tn, name):
    M, K = x.shape
    N = w.shape[1]
    return pl.pallas_call(
        _norm_matmul_kernel,
        grid=(M // tm, N // tn),
        in_specs=[pl.BlockSpec((tm, K), lambda i, j: (i, 0)),
                  pl.BlockSpec((1, K), lambda i, j: (0, 0)),
                  pl.BlockSpec((K, tn), lambda i, j: (0, j))],
        out_specs=pl.BlockSpec((tm, tn), lambda i, j: (i, j)),
        out_shape=jax.ShapeDtypeStruct((M, N), F32),
        scratch_shapes=[pltpu.VMEM((tm, K), BF16)],
        compiler_params=_cp("parallel", "arbitrary"),
        name=name,
    )(x, g.reshape(1, K), w)


def _norm_matmul_t_kernel(x_ref, g_ref, w_ref, o_ref, xn_ref):
    @pl.when(pl.program_id(1) == 0)
    def _():
        x = x_ref[...]
        ms = jnp.mean(x * x, axis=-1, keepdims=True)
        xn_ref[...] = (x * lax.rsqrt(ms + EPS) * g_ref[...]).astype(BF16)

    o_ref[...] = _dot_nt(xn_ref[...], w_ref[0].astype(BF16))


def norm_matmul_t(x, g, wT, layer, n_cols, tm, tn, name):
    M, K = x.shape
    assert n_cols % tn == 0
    return pl.pallas_call(
        _norm_matmul_t_kernel,
        grid=(M // tm, n_cols // tn),
        in_specs=[pl.BlockSpec((tm, K), lambda i, j: (i, 0)),
                  pl.BlockSpec((1, K), lambda i, j: (0, 0)),
                  pl.BlockSpec((1, tn, K), lambda i, j: (layer, j, 0))],
        out_specs=pl.BlockSpec((tm, tn), lambda i, j: (i, j)),
        out_shape=jax.ShapeDtypeStruct((M, n_cols), F32),
        scratch_shapes=[pltpu.VMEM((tm, K), BF16)],
        compiler_params=_cp("parallel", "arbitrary"),
        name=name,
    )(x, g.reshape(1, K), wT)


def _dt_kernel(x_ref, g_ref, w_ref, b_ref, o_ref):
    x = x_ref[...]
    ms = jnp.mean(x * x, axis=-1, keepdims=True)
    xn = x * lax.rsqrt(ms + EPS) * g_ref[...]
    xh = xn.astype(BF16)
    xl = (xn - xh.astype(F32)).astype(BF16)
    w = w_ref[0]
    wh = w.astype(BF16)
    wl = (w - wh.astype(F32)).astype(BF16)
    v = _dot_nt(xh, wh) + _dot_nt(xl, wh) + _dot_nt(xh, wl) + b_ref[...]
    o_ref[...] = jnp.maximum(v, 0.0) + jnp.log(1.0 + jnp.exp(-jnp.abs(v)))


def dt_proj(x, g, wT, layer, col0, bias, tm, name):
    M, K = x.shape
    H = bias.shape[0]
    assert col0 % H == 0 and wT.shape[1] == col0 + H
    return pl.pallas_call(
        _dt_kernel,
        grid=(M // tm,),
        in_specs=[pl.BlockSpec((tm, K), lambda i: (i, 0)),
                  pl.BlockSpec((1, K), lambda i: (0, 0)),
                  pl.BlockSpec((1, H, K), lambda i: (layer, col0 // H, 0)),
                  pl.BlockSpec((1, H), lambda i: (0, 0))],
        out_specs=pl.BlockSpec((tm, H), lambda i: (i, 0)),
        out_shape=jax.ShapeDtypeStruct((M, H), F32),
        compiler_params=_cp("parallel"),
        name=name,
    )(x, g.reshape(1, K), wT, bias.astype(F32).reshape(1, H))


def _mmres_kernel(*refs, gated, nk):
    if gated:
        a_ref, gate_ref, w_ref, r_ref, o_ref, acc = refs
    else:
        a_ref, w_ref, r_ref, o_ref, acc = refs
    k = pl.program_id(2)

    @pl.when(k == 0)
    def _():
        acc[...] = jnp.zeros_like(acc)

    a = a_ref[...]
    if gated:
        a = (a * _silu(gate_ref[...])).astype(BF16)
    acc[...] += _dot(a, w_ref[...])

    @pl.when(k == nk - 1)
    def _():
        o_ref[...] = r_ref[...] + acc[...]


def matmul_res(a, w, res, tm, tn, tk, name, gate=None, gate_col0=0):
    M, K = a.shape
    N = w.shape[1]
    nk = K // tk
    gated = gate is not None
    in_specs = [pl.BlockSpec((tm, tk), lambda i, j, k: (i, k))]
    args = [a]
    if gated:
        goff = gate_col0 // tk
        assert goff * tk == gate_col0
        in_specs.append(pl.BlockSpec((tm, tk), lambda i, j, k: (i, goff + k)))
        args.append(gate)
    in_specs += [pl.BlockSpec((tk, tn), lambda i, j, k: (k, j)),
                 pl.BlockSpec((tm, tn), lambda i, j, k: (i, j))]
    args += [w, res]
    return pl.pallas_call(
        functools.partial(_mmres_kernel, gated=gated, nk=nk),
        grid=(M // tm, N // tn, nk),
        in_specs=in_specs,
        out_specs=pl.BlockSpec((tm, tn), lambda i, j, k: (i, j)),
        out_shape=jax.ShapeDtypeStruct((M, N), F32),
        scratch_shapes=[pltpu.VMEM((tm, tn), F32)],
        compiler_params=_cp("parallel", "parallel", "arbitrary"),
        name=name,
    )(*args)


def _conv_kernel(x_ref, w_ref, b_ref, o_ref, *, width, zero_head):
    x = x_ref[0]
    rid = lax.broadcasted_iota(jnp.int32, x.shape, 0)
    acc = b_ref[...] + x * w_ref[width - 1:width, :]
    for k in range(width - 1):
        s = width - 1 - k
        xs = pltpu.roll(x, s, 0)
        if zero_head:
            xs = jnp.where(rid < s, 0.0, xs)
        acc = acc + xs * w_ref[k:k + 1, :]
    o_ref[0] = _silu(acc)


def conv_silu(inp, col0, w, b, tc, zero_head, name):
    Bt, T, _ = inp.shape
    W, C = w.shape
    off = col0 // tc
    assert off * tc == col0 and C % tc == 0
    return pl.pallas_call(
        functools.partial(_conv_kernel, width=W, zero_head=zero_head),
        grid=(Bt, C // tc),
        in_specs=[pl.BlockSpec((1, T, tc), lambda b, j: (b, 0, off + j)),
                  pl.BlockSpec((W, tc), lambda b, j: (0, j)),
                  pl.BlockSpec((1, tc), lambda b, j: (0, j))],
        out_specs=pl.BlockSpec((1, T, tc), lambda b, j: (b, 0, j)),
        out_shape=jax.ShapeDtypeStruct((Bt, T, C), F32),
        compiler_params=_cp("parallel", "parallel"),
        name=name,
    )(inp, w, b.reshape(1, C))


def _ssd_kernel(*refs, gpb, n_state, carry, hpg, hd, **kw):
    gw = hpg * hd
    masks = _ssd_masks(kw.pop("seg"))

    def cols(r, w, u):
        return r.at[:, :, u * w:(u + 1) * w]

    for u in range(gpb):
        z_ref, x_ref, b_ref, c_ref, dtT_ref, acol_ref, aw_ref, dw_ref, gw_ref = refs[:9]
        views = [cols(z_ref, gw, u), cols(x_ref, gw, u), cols(b_ref, n_state, u), cols(c_ref, n_state, u),
                 dtT_ref.at[u * hpg:(u + 1) * hpg, :], acol_ref.at[u * hpg:(u + 1) * hpg, :],
                 aw_ref.at[:, u * gw:(u + 1) * gw], dw_ref.at[:, u * gw:(u + 1) * gw],
                 gw_ref.at[:, u * gw:(u + 1) * gw]]
        if carry:
            y_ref, st_ref, state = refs[9:]
            views += [cols(y_ref, gw, u), st_ref.at[:, u:u + 1], state.at[u]]
        else:
            ys_ref, y_ref = refs[9:]
            views += [cols(ys_ref, gw, u), cols(y_ref, gw, u)]
        _ssd_group(*views, masks=masks, carry=carry, hpg=hpg, hd=hd, **kw)


def _ssd_masks(seg):
    ii = lax.broadcasted_iota(jnp.int32, (CHUNK, CHUNK), 0)
    jj = lax.broadcasted_iota(jnp.int32, (CHUNK, CHUNK), 1)
    causal = jj <= ii
    upper = ii <= jj
    if seg < CHUNK:
        sh = int(math.log2(seg))
        same = (ii >> sh) == (jj >> sh)
        causal = causal & same
        upper = upper & same
    Lc = jnp.where(causal, 1.0, 0.0).astype(BF16)
    LT = jnp.where(upper, 1.0, 0.0).astype(BF16)
    eye = jnp.where(ii == jj, 1.0, 0.0).astype(BF16)
    return causal, LT, jnp.concatenate([Lc, eye], axis=0)


def _ssd_group(*refs, masks, lo, hi, tile, carry, hpg, hd, nchunk):
    if carry:
        (z_ref, x_ref, b_ref, c_ref, dtT_ref, acol_ref, aw_ref, dw_ref, gw_ref,
         y_ref, st_ref, state) = refs
    else:
        (z_ref, x_ref, b_ref, c_ref, dtT_ref, acol_ref, aw_ref, dw_ref, gw_ref,
         ys_ref, y_ref) = refs
    ci = pl.program_id(2)
    gw = hpg * hd
    x = x_ref[0]
    Bm = b_ref[0]
    Cm = c_ref[0]

    lane = lax.broadcasted_iota(jnp.int32, (1, CHUNK), 1)
    pos = ci * CHUNK + lane
    if tile is not None:
        pos = pos & (tile - 1)
    valid = (pos >= lo) & (pos < hi)
    dtT = jnp.where(valid, dtT_ref[...], 0.0)

    causal, LT, LI = masks
    parts = _split3(dtT)
    cum_row = sum(_dot(p, LT) for p in parts) * acol_ref[...]

    def widen(a):
        return jnp.broadcast_to(a[:, None, :], (hpg, hd, CHUNK)).reshape(gw, CHUNK)

    G = sum(_dot_nt(LI, widen(p.astype(F32)).astype(BF16)) for p in parts)
    cumcol = G[:CHUNK] * aw_ref[...]
    dtcol = G[CHUNK:]

    cb = _dot_nt(Cm.astype(BF16), Bm.astype(BF16))
    per = LANES // hd
    lanep = lax.broadcasted_iota(jnp.int32, (1, LANES), 1)
    ys = []
    for sp in range(gw // LANES):
        xp = x[:, sp * LANES:(sp + 1) * LANES]
        Ms, Xs = [], []
        for hh in range(per):
            h = sp * per + hh
            ccol = cumcol[:, h * hd:h * hd + 1]
            crow = cum_row[h:h + 1, :]
            dec = jnp.exp(jnp.where(causal, ccol - crow, NEG))
            Ms.append((dec * cb * dtT[h:h + 1, :]).astype(BF16))
            sel = (lanep >= hh * hd) & (lanep < (hh + 1) * hd)
            Xs.append(jnp.where(sel, xp, 0.0).astype(BF16))
        ys.append(_dot(jnp.concatenate(Ms, axis=1), jnp.concatenate(Xs, axis=0)))
    y = jnp.concatenate(ys, axis=1)

    if carry:
        @pl.when(ci == 0)
        def _():
            state[...] = jnp.zeros_like(state)

        ST = state[...]
        yst = _dot(Cm.astype(BF16), ST.astype(BF16))
    else:
        yst = ys_ref[0]
    y = y + yst * jnp.exp(cumcol)

    if carry:
        last = cumcol[CHUNK - 1:CHUNK, :]
        xw = (x * (jnp.exp(last - cumcol) * dtcol)).astype(BF16)
        new = ST * jnp.exp(last) + _dot(Bm.T.astype(BF16), xw)
        state[...] = new
        st_ref[0, 0] = new

    y = y + x * dw_ref[...]
    y = y * _silu(z_ref[0])
    ms = jnp.mean(y * y, axis=-1, keepdims=True)
    y_ref[0] = (y * lax.rsqrt(ms + EPS) * gw_ref[...]).astype(BF16)


def ssd_chunks(z3, act3, dtT, A, D, gate_norm, dims, *, seg, lo, hi, tile, ystate=None, name):
    H, hd, G, N = dims
    hpg = H // G
    gw = hpg * hd
    d_inner = H * hd
    Bt, T, _ = act3.shape
    nchunk = T // CHUNK
    carry = ystate is None
    gpb = 8 if G % 8 == 0 and (d_inner // N) % 8 == 0 else 1
    boff = d_inner // (N * gpb)
    coff = (d_inner + G * N) // (N * gpb)
    sh, sw, sn, ng = hpg * gpb, gw * gpb, N * gpb, G // gpb
    in_specs = [
        pl.BlockSpec((1, CHUNK, sw), lambda b, g, c: (b, c, g)),
        pl.BlockSpec((1, CHUNK, sw), lambda b, g, c: (b, c, g)),
        pl.BlockSpec((1, CHUNK, sn), lambda b, g, c: (b, c, boff + g)),
        pl.BlockSpec((1, CHUNK, sn), lambda b, g, c: (b, c, coff + g)),
        pl.BlockSpec((sh, CHUNK), lambda b, g, c: (g, b * nchunk + c)),
        pl.BlockSpec((sh, 1), lambda b, g, c: (g, 0)),
        pl.BlockSpec((1, sw), lambda b, g, c: (0, g)),
        pl.BlockSpec((1, sw), lambda b, g, c: (0, g)),
        pl.BlockSpec((1, sw), lambda b, g, c: (0, g)),
    ]
    args = [z3, act3, act3, act3, dtT, A.reshape(H, 1),
            jnp.repeat(A, hd).reshape(1, d_inner),
            jnp.repeat(D.astype(F32), hd).reshape(1, d_inner),
            gate_norm.astype(F32).reshape(1, d_inner)]
    y_spec = pl.BlockSpec((1, CHUNK, sw), lambda b, g, c: (b, c, g))
    y_shape = jax.ShapeDtypeStruct((Bt, T, d_inner), BF16)
    kern = functools.partial(_ssd_kernel, gpb=gpb, n_state=N, seg=seg, lo=lo, hi=hi, tile=tile, carry=carry,
                             hpg=hpg, hd=hd, nchunk=nchunk)
    if carry:
        return pl.pallas_call(
            kern, grid=(Bt, ng, nchunk), in_specs=in_specs,
            out_specs=[y_spec, pl.BlockSpec((1, gpb, N, gw), lambda b, g, c: (b, g, 0, 0))],
            out_shape=[y_shape, jax.ShapeDtypeStruct((Bt, G, N, gw), F32)],
            scratch_shapes=[pltpu.VMEM((gpb, N, gw), F32)],
            compiler_params=_cp("parallel", "parallel", "arbitrary"),
            name=name,
        )(*args)
    in_specs.append(pl.BlockSpec((1, CHUNK, sw), lambda b, g, c: (b, c, g)))
    args.append(ystate)
    return pl.pallas_call(
        kern, grid=(Bt, ng, nchunk), in_specs=in_specs, out_specs=y_spec, out_shape=y_shape,
        compiler_params=_cp("parallel", "parallel", "arbitrary"),
        name=name,
    )(*args)


def _sstate_kernel(*refs, H, hd, G, N, tile, lo, chained):
    if chained:
        s0_ref, act_ref, dtT_ref, acol_ref, _, ys_ref, sn_ref, xwT = refs
    else:
        s0_ref, act_ref, dtT_ref, acol_ref, ys_ref, sn_ref, xwT = refs
    d_inner = H * hd
    gw = d_inner // G
    per_blk = CHUNK // tile
    sh = int(math.log2(tile))
    sub = pl.program_id(0) % per_blk
    r0 = pl.multiple_of(sub * tile, tile)
    lane = lax.broadcasted_iota(jnp.int32, (1, CHUNK), 1)
    valid = (lane & (tile - 1)) >= lo
    dtT = jnp.where(valid, dtT_ref[...], 0.0)
    acol = acol_ref[...]

    @pl.when(sub == 0)
    def _():
        ii = lax.broadcasted_iota(jnp.int32, (CHUNK, CHUNK), 0)
        jj = lax.broadcasted_iota(jnp.int32, (CHUNK, CHUNK), 1)
        U = jnp.where((ii > jj) & ((ii >> sh) == (jj >> sh)), 1.0, 0.0).astype(BF16)
        suf = sum(_dot(p, U) for p in _split3(dtT)) * acol
        wd = jnp.exp(suf) * dtT
        wdw = jnp.broadcast_to(wd[:, None, :], (H, hd, CHUNK)).reshape(d_inner, CHUNK)
        xT = act_ref[:, 0:d_inner].T
        xwT[...] = (xT * wdw).astype(BF16)

    own = (lane >> sh) == sub
    last = jnp.sum(jnp.where(own, dtT, 0.0), axis=1, keepdims=True) * acol
    dec = jnp.exp(last)
    dec_col = jnp.broadcast_to(dec[:, None, :], (H, hd, 1)).reshape(d_inner, 1)
    rown = (lax.broadcasted_iota(jnp.int32, (CHUNK, 1), 0) >> sh) == sub
    for g in range(G):
        S0 = s0_ref[0, 0, g * gw:(g + 1) * gw, :]
        Cg = act_ref[pl.ds(r0, tile), d_inner + G * N + g * N:d_inner + G * N + (g + 1) * N]
        ys_ref[pl.ds(r0, tile), g * gw:(g + 1) * gw] = _dot_nt(Cg, S0)
        Bg = jnp.where(rown, act_ref[:, d_inner + g * N:d_inner + (g + 1) * N], 0.0).astype(BF16)
        dS = _dot(xwT[g * gw:(g + 1) * gw, :], Bg)
        sn_ref[0, 0, g * gw:(g + 1) * gw, :] = S0 * dec_col[g * gw:(g + 1) * gw] + dS


def sample_state(s_all, layer, act2, dtT, A, dims, *, tile, lo, prev, name):
    H, hd, G, N = dims
    d_inner = H * hd
    nseq = s_all.shape[1]
    rows, cdim = act2.shape
    per_blk = CHUNK // tile
    chained = prev is not None
    in_specs = [pl.BlockSpec((1, 1, d_inner, N), lambda b: (layer, b, 0, 0)),
                pl.BlockSpec((CHUNK, cdim), lambda b: (b // per_blk, 0)),
                pl.BlockSpec((H, CHUNK), lambda b: (0, b // per_blk)),
                pl.BlockSpec((H, 1), lambda b: (0, 0))]
    args = [s_all, act2, dtT, A.reshape(H, 1)]
    if chained:
        in_specs.append(pl.BlockSpec(memory_space=pl.ANY))
        args.append(prev)
    return pl.pallas_call(
        functools.partial(_sstate_kernel, H=H, hd=hd, G=G, N=N, tile=tile, lo=lo, chained=chained),
        grid=(nseq,),
        in_specs=in_specs,
        out_specs=[pl.BlockSpec((CHUNK, d_inner), lambda b: (b // per_blk, 0)),
                   pl.BlockSpec((1, 1, d_inner, N), lambda b: (layer, b, 0, 0))],
        out_shape=[jax.ShapeDtypeStruct((rows, d_inner), F32),
                   jax.ShapeDtypeStruct(s_all.shape, F32)],
        scratch_shapes=[pltpu.VMEM((d_inner, CHUNK), BF16)],
        input_output_aliases={4: 1} if chained else {},
        compiler_params=_cp("arbitrary"),
        name=name,
    )(*args)


def _kvpost_kernel(a_ref, g_ref, cs_ref, rows_ref, kb_ref, *, lora, rope):
    a = a_ref[...]
    c = a[:, :lora]
    ms = jnp.mean(c * c, axis=-1, keepdims=True)
    cn = c * lax.rsqrt(ms + EPS) * g_ref[...]
    u = a[:, lora:lora + LANES] * cs_ref[...]
    kr = u + pltpu.roll(u, rope, 1)
    lane = lax.broadcasted_iota(jnp.int32, (1, LANES), 1)
    rows_ref[:, :lora] = cn
    rows_ref[:, lora:lora + rope] = kr[:, :rope]
    kb_ref[:, :lora] = cn.astype(BF16)
    kb_ref[:, lora:lora + LANES] = jnp.where(lane < rope, kr, 0.0).astype(BF16)


def kv_post(a, g, cs, lora, rope, tm, name):
    M = a.shape[0]
    assert 2 * rope == LANES
    wa = lora + LANES
    return pl.pallas_call(
        functools.partial(_kvpost_kernel, lora=lora, rope=rope),
        grid=(M // tm,),
        in_specs=[pl.BlockSpec((tm, wa), lambda i: (i, 0)),
                  pl.BlockSpec((1, lora), lambda i: (0, 0)),
                  pl.BlockSpec((tm, LANES), lambda i: (i, 0))],
        out_specs=[pl.BlockSpec((tm, lora + rope), lambda i: (i, 0)),
                   pl.BlockSpec((tm, wa), lambda i: (i, 0))],
        out_shape=[jax.ShapeDtypeStruct((M, lora + rope), F32),
                   jax.ShapeDtypeStruct((M, wa), BF16)],
        compiler_params=_cp("parallel"),
        name=name,
    )(a, g.reshape(1, lora), cs)


def _key_inv_rms(kb, waug, heads, nope, qk):
    n = kb.shape[0]
    kn = _dot_nt(waug, kb)
    sq = kn * kn
    ssh = jnp.sum(sq[:heads * nope].reshape(heads, nope, n), axis=1)
    ssr = jnp.sum(sq[heads * nope:], axis=0, keepdims=True)
    return lax.rsqrt((ssh + ssr) / qk + EPS)


def _rms_kernel(kb_ref, w_ref, o_ref, *, heads, nope, qk):
    o_ref[...] = _key_inv_rms(kb_ref[...], w_ref[...], heads, nope, qk)


def key_rms(kb, waug, heads, nope, qk, tk, name):
    M, W = kb.shape
    R = waug.shape[0]
    return pl.pallas_call(
        functools.partial(_rms_kernel, heads=heads, nope=nope, qk=qk),
        grid=(M // tk,),
        in_specs=[pl.BlockSpec((tk, W), lambda i: (i, 0)),
                  pl.BlockSpec((R, W), lambda i: (0, 0))],
        out_specs=pl.BlockSpec((heads, tk), lambda i: (0, i)),
        out_shape=jax.ShapeDtypeStruct((heads, M), F32),
        compiler_params=_cp("parallel"),
        name=name,
    )(kb, waug)


def _q_kernel(p_ref, g_ref, wq_ref, cs_ref, qn_ref, kn_ref, wuk_ref, o_ref, *, heads, nope, rope, lora, qk):
    p = p_ref[...]
    ms = jnp.mean(p * p, axis=-1, keepdims=True)
    qa = (p * lax.rsqrt(ms + EPS) * g_ref[...]).astype(BF16)
    lane = lax.broadcasted_iota(jnp.int32, (1, LANES), 1)
    cs = cs_ref[...]
    sc = qn_ref[...] * kn_ref[...] * (1.0 / math.sqrt(qk))
    hw = nope + LANES
    qall = _dot(qa, wq_ref[...])
    qns = []
    for h in range(heads):
        qnope = qall[:, h * hw:h * hw + nope]
        u = qall[:, h * hw + nope:(h + 1) * hw] * cs
        qr = u + pltpu.roll(u, rope, 1)
        qr = jnp.where(lane < rope, qr, 0.0)
        ss = jnp.sum(qnope * qnope, axis=-1, keepdims=True) + jnp.sum(qr * qr, axis=-1, keepdims=True)
        inv = lax.rsqrt(ss / qk + EPS)
        qns.append((qnope * inv * sc[:, :nope]).astype(BF16))
        o_ref[h, :, lora:lora + LANES] = (qr * inv * sc[:, nope:]).astype(BF16)
    for h in range(heads):
        o_ref[h, :, :lora] = _dot(qns[h], wuk_ref[h]).astype(BF16)


def q_side(proj, g, wq_ext, cs, qn_ext, kn_ext, wukT, dims, tm, name):
    heads, nope, rope, lora, qk = dims
    M = proj.shape[0]
    qlora = g.shape[0]
    hw = nope + LANES
    return pl.pallas_call(
        functools.partial(_q_kernel, heads=heads, nope=nope, rope=rope, lora=lora, qk=qk),
        grid=(M // tm,),
        in_specs=[pl.BlockSpec((tm, qlora), lambda i: (i, 0)),
                  pl.BlockSpec((1, qlora), lambda i: (0, 0)),
                  pl.BlockSpec((qlora, heads * hw), lambda i: (0, 0)),
                  pl.BlockSpec((tm, LANES), lambda i: (i, 0)),
                  pl.BlockSpec((1, hw), lambda i: (0, 0)),
                  pl.BlockSpec((1, hw), lambda i: (0, 0)),
                  pl.BlockSpec((heads, nope, lora), lambda i: (0, 0, 0))],
        out_specs=pl.BlockSpec((heads, tm, lora + LANES), lambda i: (0, i, 0)),
        out_shape=jax.ShapeDtypeStruct((heads, M, lora + LANES), BF16),
        compiler_params=_cp("parallel"),
        name=name,
    )(proj, g.reshape(1, qlora), wq_ext, cs, qn_ext, kn_ext, wukT)


def _softmax_update(s, v, m_sc, l_sc, acc_sc, rows=None, v_transposed=False):
    rows = slice(None) if rows is None else rows
    m_old = m_sc[rows]
    m_new = jnp.maximum(m_old, jnp.max(s, axis=-1, keepdims=True))
    alpha = jnp.exp(m_old - m_new)
    p = jnp.exp(s - m_new)
    l_sc[rows] = alpha * l_sc[rows] + jnp.sum(p, axis=-1, keepdims=True)
    pv = _dot_nt(p.astype(BF16), v) if v_transposed else _dot(p.astype(BF16), v)
    acc_sc[rows] = alpha * acc_sc[rows] + pv
    m_sc[rows] = m_new


def _attn_init(m_sc, l_sc, acc_sc):
    m_sc[...] = jnp.full_like(m_sc, NEG)
    l_sc[...] = jnp.zeros_like(l_sc)
    acc_sc[...] = jnp.zeros_like(acc_sc)


def _attn_finish(o_ref, wuv_ref, l_sc, acc_sc, heads, tq, vd, head0=0):
    inv = 1.0 / l_sc[...]
    for h in range(heads):
        oh = (acc_sc[h * tq:(h + 1) * tq, :] * inv[h * tq:(h + 1) * tq]).astype(BF16)
        o_ref[0, :, (head0 + h) * vd:(head0 + h + 1) * vd] = _dot(oh, wuv_ref[head0 + h])


def _pattn_kernel(q_ref, k_ref, r_ref, wuv_ref, o_ref, m_sc, l_sc, acc_sc, *, hg, tq, ck, lora, vd):
    qi = pl.program_id(1)
    gi = pl.program_id(2)
    W = q_ref.shape[-1]
    Q = q_ref[:, 0].reshape(hg * tq, W)
    _attn_init(m_sc, l_sc, acc_sc)

    def chunk(c, masked, w=ck):
        k0 = pl.multiple_of(c * ck, ck)
        Kc = k_ref[0, pl.ds(k0, w), :]
        rg = r_ref[0, c, pl.ds(pl.multiple_of(gi * hg, hg), hg), :w]
        s = _dot_nt(Q, Kc).reshape(hg, tq, w) * rg[:, None, :]
        if masked:
            ti = qi * tq + lax.broadcasted_iota(jnp.int32, (1, tq, w), 1)
            kj = k0 + lax.broadcasted_iota(jnp.int32, (1, tq, w), 2)
            s = jnp.where(kj <= ti, s, NEG)
        _softmax_update(s.reshape(hg * tq, w), Kc[:, :lora], m_sc, l_sc, acc_sc)

    nfull = (qi * tq) // ck

    def body(c, carry):
        chunk(c, False)
        return carry

    lax.fori_loop(0, nfull, body, 0)
    half = ck // 2
    if half % tq == 0:
        fits = (qi * tq - nfull * ck) + tq <= half

        @pl.when(fits)
        def _():
            chunk(nfull, True, half)

        @pl.when(jnp.logical_not(fits))
        def _():
            chunk(nfull, True)
    else:
        chunk(nfull, True)
    _attn_finish(o_ref, wuv_ref, l_sc, acc_sc, hg, tq, vd)


def prompt_attention(q4, kb3, r4, wuv, tq, ck, hg, name):
    heads, B, T, W = q4.shape
    Tk = kb3.shape[1]
    lora, vd = wuv.shape[1], wuv.shape[2]
    rows = hg * tq
    return pl.pallas_call(
        functools.partial(_pattn_kernel, hg=hg, tq=tq, ck=ck, lora=lora, vd=vd),
        grid=(B, T // tq, heads // hg),
        in_specs=[pl.BlockSpec((hg, 1, tq, W), lambda b, i, g: (g, b, i, 0)),
                  pl.BlockSpec((1, Tk, W), lambda b, i, g: (b, 0, 0)),
                  pl.BlockSpec((1, Tk // ck, heads, ck), lambda b, i, g: (b, 0, 0, 0)),
                  pl.BlockSpec((hg, lora, vd), lambda b, i, g: (g, 0, 0))],
        out_specs=pl.BlockSpec((1, tq, hg * vd), lambda b, i, g: (b, i, g)),
        out_shape=jax.ShapeDtypeStruct((B, T, heads * vd), F32),
        scratch_shapes=[pltpu.VMEM((rows, 1), F32), pltpu.VMEM((rows, 1), F32), pltpu.VMEM((rows, lora), F32)],
        compiler_params=_cp("parallel", "parallel", "arbitrary"),
        name=name,
    )(q4, kb3, r4, wuv)


def _sattn_kernel(pt_ref, q_ref, *refs, npp, nsteps, heads, tq, page, lora, vd, nope, qk, group, make_r):
    pages = refs[:npp]
    if make_r:
        wuk_ref, kn_ref, rn_ref, wuv_ref, o_ref, rp_ref, m_sc, l_sc, acc_sc, kt_sc = refs[npp:]
    else:
        rp_ref, kn_ref, rn_ref, wuv_ref, o_ref, m_sc, l_sc, acc_sc, kt_sc = refs[npp:]
    step = pl.program_id(1)

    @pl.when(step == 0)
    def _():
        _attn_init(m_sc, l_sc, acc_sc)

    Q = q_ref[0]
    ssr = []
    for i in range(npp):
        pg = pages[i][0]
        kt_sc[:, i * page:(i + 1) * page] = pg.astype(BF16)
        if make_r:
            kr = pg[lora:]
            ssr.append(jnp.sum(kr * kr, axis=0, keepdims=True))
    nk = npp * page
    if make_r:
        for g in range(0, npp, group):
            n0, n1 = g * page, (g + group) * page
            kn = _dot(wuk_ref[...], kt_sc[:lora, n0:n1])
            ssh = jnp.sum((kn * kn).reshape(heads, nope, n1 - n0), axis=1)
            rp_ref[0, :, n0:n1] = lax.rsqrt((ssh + jnp.concatenate(ssr[g:g + group], axis=1)) / qk + EPS)
    s = _dot(Q, kt_sc[...]).reshape(heads, tq, nk) * rp_ref[0][:, None, :]
    _softmax_update(s.reshape(heads * tq, nk), kt_sc[:lora, :], m_sc, l_sc, acc_sc, v_transposed=True)

    @pl.when(step == nsteps - 1)
    def _():
        Kn = kn_ref[0]
        s = _dot(Q, Kn).reshape(heads, tq, page) * rn_ref[0][:, None, :]
        ti = lax.broadcasted_iota(jnp.int32, (1, tq, page), 1)
        kj = lax.broadcasted_iota(jnp.int32, (1, tq, page), 2)
        s = jnp.where(kj <= ti, s, NEG)
        _softmax_update(s.reshape(heads * tq, page), Kn[:lora, :], m_sc, l_sc, acc_sc, v_transposed=True)
        _attn_finish(o_ref, wuv_ref, l_sc, acc_sc, heads, tq, vd)


def sample_attention(page_table, q3, cacheT, r_or_wuk, knewT, rnew, wuv, tq, npp, nope, qk, make_r, name):
    nseq, npages = page_table.shape
    _, W, page = cacheT.shape
    heads, lora, vd = wuv.shape
    nsteps = npages // npp
    group = 4 if npp % 4 == 0 else 1
    page_specs = [pl.BlockSpec((1, W, page), (lambda b, s, pt, i=i: (pt[b, s * npp + i], 0, 0)))
                  for i in range(npp)]
    r_spec = pl.BlockSpec((1, heads, npp * page), lambda b, s, pt: (b, 0, s))
    o_spec = pl.BlockSpec((1, tq, heads * vd), lambda b, s, pt: (b, 0, 0))
    o_shape = jax.ShapeDtypeStruct((nseq, tq, heads * vd), F32)
    if make_r:
        first = pl.BlockSpec(r_or_wuk.shape, lambda b, s, pt: (0, 0))
        out_specs = [o_spec, r_spec]
        out_shape = [o_shape, jax.ShapeDtypeStruct((nseq, heads, npages * page), F32)]
    else:
        first = r_spec
        out_specs, out_shape = o_spec, o_shape
    return pl.pallas_call(
        functools.partial(_sattn_kernel, npp=npp, nsteps=nsteps, heads=heads, tq=tq, page=page,
                          lora=lora, vd=vd, nope=nope, qk=qk, group=group, make_r=make_r),
        grid_spec=pltpu.PrefetchScalarGridSpec(
            num_scalar_prefetch=1,
            grid=(nseq, nsteps),
            in_specs=[pl.BlockSpec((1, heads * tq, W), lambda b, s, pt: (b, 0, 0))] + page_specs + [
                first,
                pl.BlockSpec((1, W, page), lambda b, s, pt: (b, 0, 0)),
                pl.BlockSpec((1, heads, page), lambda b, s, pt: (b, 0, 0)),
                pl.BlockSpec((heads, lora, vd), lambda b, s, pt: (0, 0, 0))],
            out_specs=out_specs,
            scratch_shapes=[pltpu.VMEM((heads * tq, 1), F32), pltpu.VMEM((heads * tq, 1), F32),
                            pltpu.VMEM((heads * tq, lora), F32), pltpu.VMEM((W, npp * page), BF16)],
        ),
        out_shape=out_shape,
        compiler_params=_cp("parallel", "arbitrary"),
        name=name,
    )(page_table, q3, *([cacheT] * npp), r_or_wuk, knewT, rnew, wuv)


def _rope_table(pos, rope):
    inv = 1.0 / (ROPE_BASE ** (jnp.arange(0, rope, 2, dtype=F32) / rope))
    f = pos.astype(F32)[:, None] * inv[None, :]
    emb = jnp.concatenate([f, f], -1)
    return jnp.concatenate([jnp.cos(emb), jnp.sin(emb)], -1)


def _rot_cols(w, rope):
    h = rope // 2
    return jnp.concatenate([-w[..., h:], w[..., :h]], -1)


def kernel(x_prompt, x_sample, state_ssm, state_conv, cache_kv, page_table, meta_tokens,
           a_norm, a_w_in, a_conv_w, a_conv_b, a_dt_bias, a_A_log, a_D, a_gate_norm, a_w_out,
           kv_norm, w_kv_a, kv_a_norm, w_uk, w_uv, k_norm,
           b_norm, b_w_in, b_q_a_norm, b_w_q, b_q_norm, b_w_out):
    bt, L, dm = x_prompt.shape
    nseq, S, _ = x_sample.shape
    n_meta = meta_tokens.shape[0]
    n_a = a_w_in.shape[0]
    n_b = b_w_in.shape[0]
    _, _, H, hd, N = state_ssm.shape
    cw, cdim = a_conv_w.shape[1:]
    d_inner = a_w_out.shape[1]
    G = (cdim - d_inner) // (2 * N)
    sdims = (H, hd, G, N)
    lora, heads, nope = w_uk.shape
    vd = w_uv.shape[2]
    kvrow = cache_kv.shape[2]
    rope = kvrow - lora
    qk = nope + rope
    qlora = b_q_a_norm.shape[1]
    page = cache_kv.shape[1]
    past = page_table.shape[1] * page
    tile = SUBLANES
    assert cw - 1 + S + 1 == tile and CHUNK % tile == 0 and (nseq * tile) % CHUNK == 0

    T = L + n_meta
    Tp = -(-T // CHUNK) * CHUNK
    Mp = bt * Tp
    Ms = nseq * S
    tmp = _pick(Mp, 512)
    tms = _pick(Ms, 512)

    xp = jnp.concatenate([jnp.broadcast_to(meta_tokens[None], (bt, n_meta, dm)), x_prompt,
                          jnp.zeros((bt, Tp - T, dm), F32)], 1).reshape(Mp, dm)
    xs = x_sample.reshape(Ms, dm)

    ssm_p, conv_p, conv_s = [], [], []
    s_new = None
    a_w_inT = jnp.transpose(a_w_in, (0, 2, 1))
    for i in range(n_a):
        w_out = a_w_out[i].astype(BF16)
        A = -jnp.exp(a_A_log[i].astype(F32))
        tn = _pick(d_inner + cdim, 512)
        tmw = _pick(Mp, 1088)

        proj = norm_matmul_t(xp, a_norm[i], a_w_inT, i, d_inner + cdim, tmw, tn, name=f"a{i}_in_p")
        dt = dt_proj(xp, a_norm[i], a_w_inT, i, d_inner + cdim, a_dt_bias[i], tmp, name=f"a{i}_dt_p")
        proj3 = proj.reshape(bt, Tp, d_inner + cdim)
        act = conv_silu(proj3, d_inner, a_conv_w[i], a_conv_b[i], 256,True, name=f"a{i}_conv_p")
        y, st = ssd_chunks(proj3, act, dt.T, A, a_D[i], a_gate_norm[i], sdims,
                           seg=CHUNK, lo=0, hi=T, tile=None, name=f"a{i}_ssd_p")
        xp = matmul_res(y.reshape(Mp, d_inner), w_out, xp, tmw, 512, d_inner, name=f"a{i}_out_p")
        conv_p.append(proj3[:, T - (cw - 1):T, d_inner:])
        ssm_p.append(st.reshape(bt, G, N, H // G, hd).transpose(0, 1, 3, 4, 2).reshape(bt, H, hd, N))

        proj_s = norm_matmul_t(xs, a_norm[i], a_w_inT, i, d_inner + cdim, tms, tn, name=f"a{i}_in_s")
        dt_s = dt_proj(xs, a_norm[i], a_w_inT, i, d_inner + cdim, a_dt_bias[i], tms, name=f"a{i}_dt_s")
        ps3 = proj_s.reshape(nseq, S, d_inner + cdim)
        z8 = jnp.concatenate([jnp.zeros((nseq, tile - S, d_inner), F32), ps3[..., :d_inner]], 1)
        full8 = jnp.concatenate([jnp.zeros((nseq, 1, cdim), F32), state_conv[i], ps3[..., d_inner:]], 1)
        conv_s.append(full8[:, tile - (cw - 1):])
        dt8T = jnp.concatenate([jnp.zeros((nseq, tile - S, H), F32), dt_s.reshape(nseq, S, H)],
                               1).reshape(nseq * tile, H).T
        act8 = conv_silu(full8.reshape(1, nseq * tile, cdim), 0, a_conv_w[i], a_conv_b[i], 256,False,
                         name=f"a{i}_conv_s")
        ys, s_new = sample_state(state_ssm.reshape(n_a, nseq, d_inner, N), i, act8[0], dt8T, A, sdims,
                                 tile=tile, lo=tile - S, prev=s_new, name=f"a{i}_state_s")
        nblk = nseq * tile // CHUNK
        y8 = ssd_chunks(z8.reshape(nblk, CHUNK, d_inner), act8.reshape(nblk, CHUNK, cdim), dt8T, A, a_D[i],
                        a_gate_norm[i], sdims, seg=tile, lo=tile - S, hi=tile, tile=tile,
                        ystate=ys.reshape(nblk, CHUNK, d_inner), name=f"a{i}_ssd_s")
        y_s = y8.reshape(nseq, tile, d_inner)[:, tile - S:].reshape(Ms, d_inner)
        xs = matmul_res(y_s, w_out, xs, tms, dm, 512, name=f"a{i}_out_s")

    wa = lora + LANES
    w_kv_ext = jnp.concatenate([w_kv_a, _rot_cols(w_kv_a[:, lora:], rope)], 1).astype(BF16)
    cs_p = jnp.tile(_rope_table(jnp.arange(Tp), rope), (bt, 1))
    cs_s = jnp.tile(_rope_table(past + jnp.arange(S), rope), (nseq, 1))
    wukT2 = w_uk.transpose(1, 2, 0).reshape(heads * nope, lora)
    waug = jnp.zeros((heads * nope + rope, wa), F32)
    waug = waug.at[:heads * nope, :lora].set(wukT2)
    waug = waug.at[heads * nope:, lora:lora + rope].set(jnp.eye(rope, dtype=F32)).astype(BF16)
    wukT = w_uk.transpose(1, 2, 0).astype(BF16)
    wuv = w_uv.transpose(1, 0, 2).astype(BF16)

    a_p = norm_matmul(xp, kv_norm, w_kv_ext, tmp, wa, name="kv_a_p")
    rows_p, kb_p = kv_post(a_p, kv_a_norm, cs_p, lora, rope, tmp, name="kv_post_p")
    rT_p = key_rms(kb_p, waug, heads, nope, qk, _pick(Mp, 256), name="kv_rms_p")
    a_s = norm_matmul(xs, kv_norm, w_kv_ext, tms, wa, name="kv_a_s")
    rows_s, kb_s = kv_post(a_s, kv_a_norm, cs_s, lora, rope, tms, name="kv_post_s")
    rT_s = key_rms(kb_s, waug, heads, nope, qk, _pick(Ms, 256), name="kv_rms_s")
    npp = min(32, page_table.shape[1])
    cacheT = jnp.transpose(cache_kv, (0, 2, 1))
    r_past = None

    ck = 4 * CHUNK
    Tk = -(-Tp // ck) * ck
    kb3 = jnp.pad(kb_p.reshape(bt, Tp, wa), ((0, 0), (0, Tk - Tp), (0, 0)))
    r4 = jnp.pad(rT_p.reshape(heads, bt, Tp), ((0, 0), (0, 0), (0, Tk - Tp)), constant_values=1.0)
    r4 = r4.reshape(heads, bt, Tk // ck, ck).transpose(1, 2, 0, 3)
    knew = jnp.pad(kb_s[:, :kvrow].reshape(nseq, S, kvrow), ((0, 0), (0, page - S), (0, 0))).transpose(0, 2, 1)
    rnew = jnp.pad(rT_s.reshape(heads, nseq, S).transpose(1, 0, 2), ((0, 0), (0, 0), (0, page - S)),
                   constant_values=1.0)

    def ext(v):
        return jnp.pad(v.astype(F32), (0, LANES - rope)).reshape(1, nope + LANES)

    qdims = (heads, nope, rope, lora, qk)
    tq = CHUNK
    for j in range(n_b):
        w_in = b_w_in[j].astype(BF16)
        wq3 = b_w_q[j].reshape(qlora, heads, qk)
        wq_ext = jnp.concatenate([wq3, _rot_cols(wq3[..., nope:], rope)], -1).reshape(qlora, heads * (nope + LANES))
        wq_ext = wq_ext.astype(BF16)
        w_out = b_w_out[j].astype(BF16)
        tn = _pick(w_in.shape[1], 1280)

        tmw = _pick(Mp, 1088)
        proj = norm_matmul(xp, b_norm[j], w_in, tmw, _pick(w_in.shape[1], 640), name=f"b{j}_in_p")
        q = q_side(proj, b_q_a_norm[j], wq_ext, cs_p, ext(b_q_norm[j]), ext(k_norm), wukT, qdims,
                   _pick(Mp, 256), name=f"b{j}_q_p")
        o = prompt_attention(q.reshape(heads, bt, Tp, wa), kb3, r4, wuv, tq, ck, min(16, heads),
                             name=f"b{j}_attn_p")
        xp = matmul_res(o.reshape(Mp, heads * vd), w_out, xp, tmw, _pick(dm, 1024), 512, name=f"b{j}_out_p",
                        gate=proj, gate_col0=qlora)

        proj_s = norm_matmul(xs, b_norm[j], w_in, tms, tn, name=f"b{j}_in_s")
        q_s = q_side(proj_s, b_q_a_norm[j], wq_ext, cs_s, ext(b_q_norm[j]), ext(k_norm), wukT, qdims,
                     _pick(Ms, 256), name=f"b{j}_q_s")
        q8 = jnp.pad(q_s[..., :kvrow].reshape(heads, nseq, S, kvrow), ((0, 0), (0, 0), (0, tile - S), (0, 0)))
        q8 = q8.transpose(1, 0, 2, 3).reshape(nseq, heads * tile, kvrow)
        if r_past is None:
            o_s, r_past = sample_attention(page_table, q8, cacheT, wukT2.astype(BF16), knew, rnew, wuv, tile, npp,
                                           nope, qk, True, name=f"b{j}_attn_s")
        else:
            o_s = sample_attention(page_table, q8, cacheT, r_past, knew, rnew, wuv, tile, npp,
                                   nope, qk, False, name=f"b{j}_attn_s")
        xs = matmul_res(o_s[:, :S].reshape(Ms, heads * vd), w_out, xs, tms, dm, 512, name=f"b{j}_out_s",
                        gate=proj_s, gate_col0=qlora)

    y_prompt = xp.reshape(bt, Tp, dm)[:, n_meta:T]
    kv_p = rows_p.reshape(bt, Tp, kvrow)[:, :T]
    return (y_prompt, xs.reshape(nseq, S, dm), jnp.stack(ssm_p), jnp.stack(conv_p), kv_p,
            s_new.reshape(n_a, nseq, H, hd, N), jnp.stack(conv_s), rows_s.reshape(nseq, S, kvrow))
```

```python
import functools
import math

import jax
import jax.numpy as jnp
from jax import lax
from jax.experimental import pallas as pl
from jax.experimental.pallas import tpu as pltpu

F32 = jnp.float32
BF16 = jnp.bfloat16
EPS = 1e-6
ROPE_BASE = 10000.0
LANES = 128
SUBLANES = 8
CHUNK = 128
NEG = -1e30
VMEM_LIMIT = 48 * 1024 * 1024


def _cp(*sem):
    return pltpu.CompilerParams(dimension_semantics=sem, vmem_limit_bytes=VMEM_LIMIT)


def _dot(a, b):
    return jnp.dot(a, b, preferred_element_type=F32)


def _dot_nt(a, b):
    return lax.dot_general(a, b, (((1,), (1,)), ((), ())), preferred_element_type=F32)


def _split3(a):
    h = a.astype(BF16)
    r = a - h.astype(F32)
    m = r.astype(BF16)
    l = (r - m.astype(F32)).astype(BF16)
    return h, m, l


def _silu(v):
    return v * jax.nn.sigmoid(v)


def _pick(n, target):
    best = None
    for t in range(8, min(n, target) + 1, 8):
        if n % t == 0:
            best = t
    assert best is not None, (n, target)
    return best


def _norm_matmul_kernel(x_ref, g_ref, w_ref, o_ref, xn_ref):
    @pl.when(pl.program_id(1) == 0)
    def _():
        x = x_ref[...]
        ms = jnp.mean(x * x, axis=-1, keepdims=True)
        xn_ref[...] = (x * lax.rsqrt(ms + EPS) * g_ref[...]).astype(BF16)

    o_ref[...] = _dot(xn_ref[...], w_ref[...])


def norm_matmul(x, g, w, tm, tn, name):
    M, K = x.shape
    N = w.shape[1]
    return pl.pallas_call(
        _norm_matmul_kernel,
        grid=(M // tm, N // tn),
        in_specs=[pl.BlockSpec((tm, K), lambda i, j: (i, 0)),
                  pl.BlockSpec((1, K), lambda i, j: (0, 0)),
                  pl.BlockSpec((K, tn), lambda i, j: (0, j))],
        out_specs=pl.BlockSpec((tm, tn), lambda i, j: (i, j)),
        out_shape=jax.ShapeDtypeStruct((M, N), F32),
        scratch_shapes=[pltpu.VMEM((tm, K), BF16)],
        compiler_params=_cp("parallel", "arbitrary"),
        name=name,
    )(x, g.reshape(1, K), w)


def _norm_matmul_t_kernel(x_ref, g_ref, w_ref, o_ref, xn_ref):
    @pl.when(pl.program_id(1) == 0)
    def _():
        x = x_ref[...]
        ms = jnp.mean(x * x, axis=-1, keepdims=True)
        xn_ref[...] = (x * lax.rsqrt(ms + EPS) * g_ref[...]).astype(BF16)

    o_ref[...] = _dot_nt(xn_ref[...], w_ref[0])


def norm_matmul_t(x, g, wT, layer, n_cols, tm, tn, name):
    M, K = x.shape
    assert n_cols % tn == 0
    return pl.pallas_call(
        _norm_matmul_t_kernel,
        grid=(M // tm, n_cols // tn),
        in_specs=[pl.BlockSpec((tm, K), lambda i, j: (i, 0)),
                  pl.BlockSpec((1, K), lambda i, j: (0, 0)),
                  pl.BlockSpec((1, tn, K), lambda i, j: (layer, j, 0))],
        out_specs=pl.BlockSpec((tm, tn), lambda i, j: (i, j)),
        out_shape=jax.ShapeDtypeStruct((M, n_cols), F32),
        scratch_shapes=[pltpu.VMEM((tm, K), BF16)],
        compiler_params=_cp("parallel", "arbitrary"),
        name=name,
    )(x, g.reshape(1, K), wT)


def _dt_kernel(x_ref, g_ref, w_ref, b_ref, o_ref):
    x = x_ref[...]
    ms = jnp.mean(x * x, axis=-1, keepdims=True)
    xn = x * lax.rsqrt(ms + EPS) * g_ref[...]
    xh = xn.astype(BF16)
    xl = (xn - xh.astype(F32)).astype(BF16)
    w = w_ref[0]
    wh = w.astype(BF16)
    wl = (w - wh.astype(F32)).astype(BF16)
    v = _dot_nt(xh, wh) + _dot_nt(xl, wh) + _dot_nt(xh, wl) + b_ref[...]
    o_ref[...] = jnp.maximum(v, 0.0) + jnp.log(1.0 + jnp.exp(-jnp.abs(v)))


def dt_proj(x, g, wT, layer, col0, bias, tm, name):
    M, K = x.shape
    H = bias.shape[0]
    assert col0 % H == 0 and wT.shape[1] == col0 + H
    return pl.pallas_call(
        _dt_kernel,
        grid=(M // tm,),
        in_specs=[pl.BlockSpec((tm, K), lambda i: (i, 0)),
                  pl.BlockSpec((1, K), lambda i: (0, 0)),
                  pl.BlockSpec((1, H, K), lambda i: (layer, col0 // H, 0)),
                  pl.BlockSpec((1, H), lambda i: (0, 0))],
        out_specs=pl.BlockSpec((tm, H), lambda i: (i, 0)),
        out_shape=jax.ShapeDtypeStruct((M, H), F32),
        compiler_params=_cp("parallel"),
        name=name,
    )(x, g.reshape(1, K), wT, bias.astype(F32).reshape(1, H))


def _mmres_kernel(*refs, gated, nk):
    if gated:
        a_ref, gate_ref, w_ref, r_ref, o_ref, acc = refs
    else:
        a_ref, w_ref, r_ref, o_ref, acc = refs
    k = pl.program_id(2)

    @pl.when(k == 0)
    def _():
        acc[...] = jnp.zeros_like(acc)

    a = a_ref[...]
    if gated:
        a = (a * _silu(gate_ref[...])).astype(BF16)
    acc[...] += _dot(a, w_ref[...])

    @pl.when(k == nk - 1)
    def _():
        o_ref[...] = r_ref[...] + acc[...]


def matmul_res(a, w, res, tm, tn, tk, name, gate=None, gate_col0=0):
    M, K = a.shape
    N = w.shape[1]
    nk = K // tk
    gated = gate is not None
    in_specs = [pl.BlockSpec((tm, tk), lambda i, j, k: (i, k))]
    args = [a]
    if gated:
        goff = gate_col0 // tk
        assert goff * tk == gate_col0
        in_specs.append(pl.BlockSpec((tm, tk), lambda i, j, k: (i, goff + k)))
        args.append(gate)
    in_specs += [pl.BlockSpec((tk, tn), lambda i, j, k: (k, j)),
                 pl.BlockSpec((tm, tn), lambda i, j, k: (i, j))]
    args += [w, res]
    return pl.pallas_call(
        functools.partial(_mmres_kernel, gated=gated, nk=nk),
        grid=(M // tm, N // tn, nk),
        in_specs=in_specs,
        out_specs=pl.BlockSpec((tm, tn), lambda i, j, k: (i, j)),
        out_shape=jax.ShapeDtypeStruct((M, N), F32),
        scratch_shapes=[pltpu.VMEM((tm, tn), F32)],
        compiler_params=_cp("parallel", "parallel", "arbitrary"),
        name=name,
    )(*args)


def _conv_kernel(x_ref, w_ref, b_ref, o_ref, *, width, zero_head):
    x = x_ref[0]
    rid = lax.broadcasted_iota(jnp.int32, x.shape, 0)
    acc = b_ref[...] + x * w_ref[width - 1:width, :]
    for k in range(width - 1):
        s = width - 1 - k
        xs = pltpu.roll(x, s, 0)
        if zero_head:
            xs = jnp.where(rid < s, 0.0, xs)
        acc = acc + xs * w_ref[k:k + 1, :]
    o_ref[0] = _silu(acc)


def conv_silu(inp, col0, w, b, tc, zero_head, name):
    Bt, T, _ = inp.shape
    W, C = w.shape
    off = col0 // tc
    assert off * tc == col0 and C % tc == 0
    return pl.pallas_call(
        functools.partial(_conv_kernel, width=W, zero_head=zero_head),
        grid=(Bt, C // tc),
        in_specs=[pl.BlockSpec((1, T, tc), lambda b, j: (b, 0, off + j)),
                  pl.BlockSpec((W, tc), lambda b, j: (0, j)),
                  pl.BlockSpec((1, tc), lambda b, j: (0, j))],
        out_specs=pl.BlockSpec((1, T, tc), lambda b, j: (b, 0, j)),
        out_shape=jax.ShapeDtypeStruct((Bt, T, C), F32),
        compiler_params=_cp("parallel", "parallel"),
        name=name,
    )(inp, w, b.reshape(1, C))


def _ssd_kernel(*refs, gpb, n_state, carry, hpg, hd, **kw):
    gw = hpg * hd
    masks = _ssd_masks(kw.pop("seg"))

    def cols(r, w, u):
        return r.at[:, :, u * w:(u + 1) * w]

    for u in range(gpb):
        z_ref, x_ref, b_ref, c_ref, dtT_ref, acol_ref, aw_ref, dw_ref, gw_ref = refs[:9]
        views = [cols(z_ref, gw, u), cols(x_ref, gw, u), cols(b_ref, n_state, u), cols(c_ref, n_state, u),
                 dtT_ref.at[u * hpg:(u + 1) * hpg, :], acol_ref.at[u * hpg:(u + 1) * hpg, :],
                 aw_ref.at[:, u * gw:(u + 1) * gw], dw_ref.at[:, u * gw:(u + 1) * gw],
                 gw_ref.at[:, u * gw:(u + 1) * gw]]
        if carry:
            y_ref, st_ref, state = refs[9:]
            views += [cols(y_ref, gw, u), st_ref.at[:, u:u + 1], state.at[u]]
        else:
            ys_ref, y_ref = refs[9:]
            views += [cols(ys_ref, gw, u), cols(y_ref, gw, u)]
        _ssd_group(*views, masks=masks, carry=carry, hpg=hpg, hd=hd, **kw)


def _ssd_masks(seg):
    ii = lax.broadcasted_iota(jnp.int32, (CHUNK, CHUNK), 0)
    jj = lax.broadcasted_iota(jnp.int32, (CHUNK, CHUNK), 1)
    causal = jj <= ii
    upper = ii <= jj
    if seg < CHUNK:
        sh = int(math.log2(seg))
        same = (ii >> sh) == (jj >> sh)
        causal = causal & same
        upper = upper & same
    Lc = jnp.where(causal, 1.0, 0.0).astype(BF16)
    LT = jnp.where(upper, 1.0, 0.0).astype(BF16)
    eye = jnp.where(ii == jj, 1.0, 0.0).astype(BF16)
    return causal, LT, jnp.concatenate([Lc, eye], axis=0)


def _ssd_group(*refs, masks, lo, hi, tile, carry, hpg, hd, nchunk):
    if carry:
        (z_ref, x_ref, b_ref, c_ref, dtT_ref, acol_ref, aw_ref, dw_ref, gw_ref,
         y_ref, st_ref, state) = refs
    else:
        (z_ref, x_ref, b_ref, c_ref, dtT_ref, acol_ref, aw_ref, dw_ref, gw_ref,
         ys_ref, y_ref) = refs
    ci = pl.program_id(2)
    gw = hpg * hd
    x = x_ref[0]
    Bm = b_ref[0]
    Cm = c_ref[0]

    lane = lax.broadcasted_iota(jnp.int32, (1, CHUNK), 1)
    pos = ci * CHUNK + lane
    if tile is not None:
        pos = pos & (tile - 1)
    valid = (pos >= lo) & (pos < hi)
    dtT = jnp.where(valid, dtT_ref[...], 0.0)

    causal, LT, LI = masks
    parts = _split3(dtT)
    cum_row = sum(_dot(p, LT) for p in parts) * acol_ref[...]

    def widen(a):
        return jnp.broadcast_to(a[:, None, :], (hpg, hd, CHUNK)).reshape(gw, CHUNK)

    G = sum(_dot_nt(LI, widen(p.astype(F32)).astype(BF16)) for p in parts)
    cumcol = G[:CHUNK] * aw_ref[...]
    dtcol = G[CHUNK:]

    cb = _dot_nt(Cm.astype(BF16), Bm.astype(BF16))
    per = LANES // hd
    lanep = lax.broadcasted_iota(jnp.int32, (1, LANES), 1)
    ys = []
    for sp in range(gw // LANES):
        xp = x[:, sp * LANES:(sp + 1) * LANES]
        Ms, Xs = [], []
        for hh in range(per):
            h = sp * per + hh
            ccol = cumcol[:, h * hd:h * hd + 1]
            crow = cum_row[h:h + 1, :]
            dec = jnp.exp(jnp.where(causal, ccol - crow, NEG))
            Ms.append((dec * cb * dtT[h:h + 1, :]).astype(BF16))
            sel = (lanep >= hh * hd) & (lanep < (hh + 1) * hd)
            Xs.append(jnp.where(sel, xp, 0.0).astype(BF16))
        ys.append(_dot(jnp.concatenate(Ms, axis=1), jnp.concatenate(Xs, axis=0)))
    y = jnp.concatenate(ys, axis=1)

    if carry:
        @pl.when(ci == 0)
        def _():
            state[...] = jnp.zeros_like(state)

        ST = state[...]
        yst = _dot(Cm.astype(BF16), ST.astype(BF16))
    else:
        yst = ys_ref[0]
    y = y + yst * jnp.exp(cumcol)

    if carry:
        last = cumcol[CHUNK - 1:CHUNK, :]
        xw = (x * (jnp.exp(last - cumcol) * dtcol)).astype(BF16)
        new = ST * jnp.exp(last) + _dot(Bm.T.astype(BF16), xw)
        state[...] = new
        st_ref[0, 0] = new

    y = y + x * dw_ref[...]
    y = y * _silu(z_ref[0])
    ms = jnp.mean(y * y, axis=-1, keepdims=True)
    y_ref[0] = (y * lax.rsqrt(ms + EPS) * gw_ref[...]).astype(BF16)


def ssd_chunks(z3, act3, dtT, A, D, gate_norm, dims, *, seg, lo, hi, tile, ystate=None, name):
    H, hd, G, N = dims
    hpg = H // G
    gw = hpg * hd
    d_inner = H * hd
    Bt, T, _ = act3.shape
    nchunk = T // CHUNK
    carry = ystate is None
    gpb = 8 if G % 8 == 0 and (d_inner // N) % 8 == 0 else 1
    boff = d_inner // (N * gpb)
    coff = (d_inner + G * N) // (N * gpb)
    sh, sw, sn, ng = hpg * gpb, gw * gpb, N * gpb, G // gpb
    in_specs = [
        pl.BlockSpec((1, CHUNK, sw), lambda b, g, c: (b, c, g)),
        pl.BlockSpec((1, CHUNK, sw), lambda b, g, c: (b, c, g)),
        pl.BlockSpec((1, CHUNK, sn), lambda b, g, c: (b, c, boff + g)),
        pl.BlockSpec((1, CHUNK, sn), lambda b, g, c: (b, c, coff + g)),
        pl.BlockSpec((sh, CHUNK), lambda b, g, c: (g, b * nchunk + c)),
        pl.BlockSpec((sh, 1), lambda b, g, c: (g, 0)),
        pl.BlockSpec((1, sw), lambda b, g, c: (0, g)),
        pl.BlockSpec((1, sw), lambda b, g, c: (0, g)),
        pl.BlockSpec((1, sw), lambda b, g, c: (0, g)),
    ]
    args = [z3, act3, act3, act3, dtT, A.reshape(H, 1),
            jnp.repeat(A, hd).reshape(1, d_inner),
            jnp.repeat(D.astype(F32), hd).reshape(1, d_inner),
            gate_norm.astype(F32).reshape(1, d_inner)]
    y_spec = pl.BlockSpec((1, CHUNK, sw), lambda b, g, c: (b, c, g))
    y_shape = jax.ShapeDtypeStruct((Bt, T, d_inner), BF16)
    kern = functools.partial(_ssd_kernel, gpb=gpb, n_state=N, seg=seg, lo=lo, hi=hi, tile=tile, carry=carry,
                             hpg=hpg, hd=hd, nchunk=nchunk)
    if carry:
        return pl.pallas_call(
            kern, grid=(Bt, ng, nchunk), in_specs=in_specs,
            out_specs=[y_spec, pl.BlockSpec((1, gpb, N, gw), lambda b, g, c: (b, g, 0, 0))],
            out_shape=[y_shape, jax.ShapeDtypeStruct((Bt, G, N, gw), F32)],
            scratch_shapes=[pltpu.VMEM((gpb, N, gw), F32)],
            compiler_params=_cp("parallel", "parallel", "arbitrary"),
            name=name,
        )(*args)
    in_specs.append(pl.BlockSpec((1, CHUNK, sw), lambda b, g, c: (b, c, g)))
    args.append(ystate)
    return pl.pallas_call(
        kern, grid=(Bt, ng, nchunk), in_specs=in_specs, out_specs=y_spec, out_shape=y_shape,
        compiler_params=_cp("parallel", "parallel", "arbitrary"),
        name=name,
    )(*args)


def _sstate_kernel(*refs, H, hd, G, N, tile, lo, chained):
    if chained:
        s0_ref, act_ref, dtT_ref, acol_ref, _, ys_ref, sn_ref, xwT = refs
    else:
        s0_ref, act_ref, dtT_ref, acol_ref, ys_ref, sn_ref, xwT = refs
    d_inner = H * hd
    gw = d_inner // G
    per_blk = CHUNK // tile
    sh = int(math.log2(tile))
    sub = pl.program_id(0) % per_blk
    r0 = pl.multiple_of(sub * tile, tile)
    lane = lax.broadcasted_iota(jnp.int32, (1, CHUNK), 1)
    valid = (lane & (tile - 1)) >= lo
    dtT = jnp.where(valid, dtT_ref[...], 0.0)
    acol = acol_ref[...]

    @pl.when(sub == 0)
    def _():
        ii = lax.broadcasted_iota(jnp.int32, (CHUNK, CHUNK), 0)
        jj = lax.broadcasted_iota(jnp.int32, (CHUNK, CHUNK), 1)
        U = jnp.where((ii > jj) & ((ii >> sh) == (jj >> sh)), 1.0, 0.0).astype(BF16)
        suf = sum(_dot(p, U) for p in _split3(dtT)) * acol
        wd = jnp.exp(suf) * dtT
        wdw = jnp.broadcast_to(wd[:, None, :], (H, hd, CHUNK)).reshape(d_inner, CHUNK)
        xT = act_ref[:, 0:d_inner].T
        xwT[...] = (xT * wdw).astype(BF16)

    own = (lane >> sh) == sub
    last = jnp.sum(jnp.where(own, dtT, 0.0), axis=1, keepdims=True) * acol
    dec = jnp.exp(last)
    dec_col = jnp.broadcast_to(dec[:, None, :], (H, hd, 1)).reshape(d_inner, 1)
    rown = (lax.broadcasted_iota(jnp.int32, (CHUNK, 1), 0) >> sh) == sub
    for g in range(G):
        S0 = s0_ref[0, 0, g * gw:(g + 1) * gw, :]
        Cg = act_ref[pl.ds(r0, tile), d_inner + G * N + g * N:d_inner + G * N + (g + 1) * N]
        ys_ref[pl.ds(r0, tile), g * gw:(g + 1) * gw] = _dot_nt(Cg, S0)
        Bg = jnp.where(rown, act_ref[:, d_inner + g * N:d_inner + (g + 1) * N], 0.0).astype(BF16)
        dS = _dot(xwT[g * gw:(g + 1) * gw, :], Bg)
        sn_ref[0, 0, g * gw:(g + 1) * gw, :] = S0 * dec_col[g * gw:(g + 1) * gw] + dS


def sample_state(s_all, layer, act2, dtT, A, dims, *, tile, lo, prev, name):
    H, hd, G, N = dims
    d_inner = H * hd
    nseq = s_all.shape[1]
    rows, cdim = act2.shape
    per_blk = CHUNK // tile
    chained = prev is not None
    in_specs = [pl.BlockSpec((1, 1, d_inner, N), lambda b: (layer, b, 0, 0)),
                pl.BlockSpec((CHUNK, cdim), lambda b: (b // per_blk, 0)),
                pl.BlockSpec((H, CHUNK), lambda b: (0, b // per_blk)),
                pl.BlockSpec((H, 1), lambda b: (0, 0))]
    args = [s_all, act2, dtT, A.reshape(H, 1)]
    if chained:
        in_specs.append(pl.BlockSpec(memory_space=pl.ANY))
        args.append(prev)
    return pl.pallas_call(
        functools.partial(_sstate_kernel, H=H, hd=hd, G=G, N=N, tile=tile, lo=lo, chained=chained),
        grid=(nseq,),
        in_specs=in_specs,
        out_specs=[pl.BlockSpec((CHUNK, d_inner), lambda b: (b // per_blk, 0)),
                   pl.BlockSpec((1, 1, d_inner, N), lambda b: (layer, b, 0, 0))],
        out_shape=[jax.ShapeDtypeStruct((rows, d_inner), F32),
                   jax.ShapeDtypeStruct(s_all.shape, F32)],
        scratch_shapes=[pltpu.VMEM((d_inner, CHUNK), BF16)],
        input_output_aliases={4: 1} if chained else {},
        compiler_params=_cp("arbitrary"),
        name=name,
    )(*args)


def _kvpost_kernel(a_ref, g_ref, cs_ref, rows_ref, kb_ref, *, lora, rope):
    a = a_ref[...]
    c = a[:, :lora]
    ms = jnp.mean(c * c, axis=-1, keepdims=True)
    cn = c * lax.rsqrt(ms + EPS) * g_ref[...]
    u = a[:, lora:lora + LANES] * cs_ref[...]
    kr = u + pltpu.roll(u, rope, 1)
    lane = lax.broadcasted_iota(jnp.int32, (1, LANES), 1)
    rows_ref[:, :lora] = cn
    rows_ref[:, lora:lora + rope] = kr[:, :rope]
    kb_ref[:, :lora] = cn.astype(BF16)
    kb_ref[:, lora:lora + LANES] = jnp.where(lane < rope, kr, 0.0).astype(BF16)


def kv_post(a, g, cs, lora, rope, tm, name):
    M = a.shape[0]
    assert 2 * rope == LANES
    wa = lora + LANES
    return pl.pallas_call(
        functools.partial(_kvpost_kernel, lora=lora, rope=rope),
        grid=(M // tm,),
        in_specs=[pl.BlockSpec((tm, wa), lambda i: (i, 0)),
                  pl.BlockSpec((1, lora), lambda i: (0, 0)),
                  pl.BlockSpec((tm, LANES), lambda i: (i, 0))],
        out_specs=[pl.BlockSpec((tm, lora + rope), lambda i: (i, 0)),
                   pl.BlockSpec((tm, wa), lambda i: (i, 0))],
        out_shape=[jax.ShapeDtypeStruct((M, lora + rope), F32),
                   jax.ShapeDtypeStruct((M, wa), BF16)],
        compiler_params=_cp("parallel"),
        name=name,
    )(a, g.reshape(1, lora), cs)


def _key_inv_rms(kb, waug, heads, nope, qk):
    n = kb.shape[0]
    kn = _dot_nt(waug, kb)
    sq = kn * kn
    ssh = jnp.sum(sq[:heads * nope].reshape(heads, nope, n), axis=1)
    ssr = jnp.sum(sq[heads * nope:], axis=0, keepdims=True)
    return lax.rsqrt((ssh + ssr) / qk + EPS)


def _rms_kernel(kb_ref, w_ref, o_ref, *, heads, nope, qk):
    o_ref[...] = _key_inv_rms(kb_ref[...], w_ref[...], heads, nope, qk)


def key_rms(kb, waug, heads, nope, qk, tk, name):
    M, W = kb.shape
    R = waug.shape[0]
    return pl.pallas_call(
        functools.partial(_rms_kernel, heads=heads, nope=nope, qk=qk),
        grid=(M // tk,),
        in_specs=[pl.BlockSpec((tk, W), lambda i: (i, 0)),
                  pl.BlockSpec((R, W), lambda i: (0, 0))],
        out_specs=pl.BlockSpec((heads, tk), lambda i: (0, i)),
        out_shape=jax.ShapeDtypeStruct((heads, M), F32),
        compiler_params=_cp("parallel"),
        name=name,
    )(kb, waug)


def _q_kernel(p_ref, g_ref, wq_ref, cs_ref, qn_ref, kn_ref, wuk_ref, o_ref, *, heads, nope, rope, lora, qk):
    p = p_ref[...]
    ms = jnp.mean(p * p, axis=-1, keepdims=True)
    qa = (p * lax.rsqrt(ms + EPS) * g_ref[...]).astype(BF16)
    lane = lax.broadcasted_iota(jnp.int32, (1, LANES), 1)
    cs = cs_ref[...]
    sc = qn_ref[...] * kn_ref[...] * (1.0 / math.sqrt(qk))
    hw = nope + LANES
    qall = _dot(qa, wq_ref[...])
    qns = []
    for h in range(heads):
        qnope = qall[:, h * hw:h * hw + nope]
        u = qall[:, h * hw + nope:(h + 1) * hw] * cs
        qr = u + pltpu.roll(u, rope, 1)
        qr = jnp.where(lane < rope, qr, 0.0)
        ss = jnp.sum(qnope * qnope, axis=-1, keepdims=True) + jnp.sum(qr * qr, axis=-1, keepdims=True)
        inv = lax.rsqrt(ss / qk + EPS)
        qns.append((qnope * inv * sc[:, :nope]).astype(BF16))
        o_ref[h, :, lora:lora + LANES] = (qr * inv * sc[:, nope:]).astype(BF16)
    for h in range(heads):
        o_ref[h, :, :lora] = _dot(qns[h], wuk_ref[h]).astype(BF16)


def q_side(proj, g, wq_ext, cs, qn_ext, kn_ext, wukT, dims, tm, name):
    heads, nope, rope, lora, qk = dims
    M = proj.shape[0]
    qlora = g.shape[0]
    hw = nope + LANES
    return pl.pallas_call(
        functools.partial(_q_kernel, heads=heads, nope=nope, rope=rope, lora=lora, qk=qk),
        grid=(M // tm,),
        in_specs=[pl.BlockSpec((tm, qlora), lambda i: (i, 0)),
                  pl.BlockSpec((1, qlora), lambda i: (0, 0)),
                  pl.BlockSpec((qlora, heads * hw), lambda i: (0, 0)),
                  pl.BlockSpec((tm, LANES), lambda i: (i, 0)),
                  pl.BlockSpec((1, hw), lambda i: (0, 0)),
                  pl.BlockSpec((1, hw), lambda i: (0, 0)),
                  pl.BlockSpec((heads, nope, lora), lambda i: (0, 0, 0))],
        out_specs=pl.BlockSpec((heads, tm, lora + LANES), lambda i: (0, i, 0)),
        out_shape=jax.ShapeDtypeStruct((heads, M, lora + LANES), BF16),
        compiler_params=_cp("parallel"),
        name=name,
    )(proj, g.reshape(1, qlora), wq_ext, cs, qn_ext, kn_ext, wukT)


def _softmax_update(s, v, m_sc, l_sc, acc_sc, rows=None, v_transposed=False):
    rows = slice(None) if rows is None else rows
    m_old = m_sc[rows]
    m_new = jnp.maximum(m_old, jnp.max(s, axis=-1, keepdims=True))
    alpha = jnp.exp(m_old - m_new)
    p = jnp.exp(s - m_new)
    l_sc[rows] = alpha * l_sc[rows] + jnp.sum(p, axis=-1, keepdims=True)
    pv = _dot_nt(p.astype(BF16), v) if v_transposed else _dot(p.astype(BF16), v)
    acc_sc[rows] = alpha * acc_sc[rows] + pv
    m_sc[rows] = m_new


def _attn_init(m_sc, l_sc, acc_sc):
    m_sc[...] = jnp.full_like(m_sc, NEG)
    l_sc[...] = jnp.zeros_like(l_sc)
    acc_sc[...] = jnp.zeros_like(acc_sc)


def _attn_finish(o_ref, wuv_ref, l_sc, acc_sc, heads, tq, vd, head0=0):
    inv = 1.0 / l_sc[...]
    for h in range(heads):
        oh = (acc_sc[h * tq:(h + 1) * tq, :] * inv[h * tq:(h + 1) * tq]).astype(BF16)
        o_ref[0, :, (head0 + h) * vd:(head0 + h + 1) * vd] = _dot(oh, wuv_ref[head0 + h])


def _pattn_kernel(q_ref, k_ref, r_ref, wuv_ref, o_ref, m_sc, l_sc, acc_sc, *, hg, tq, ck, lora, vd):
    qi = pl.program_id(1)
    gi = pl.program_id(2)
    W = q_ref.shape[-1]
    Q = q_ref[:, 0].reshape(hg * tq, W)
    _attn_init(m_sc, l_sc, acc_sc)

    def chunk(c, masked, w=ck):
        k0 = pl.multiple_of(c * ck, ck)
        Kc = k_ref[0, pl.ds(k0, w), :]
        rg = r_ref[0, c, pl.ds(pl.multiple_of(gi * hg, hg), hg), :w]
        s = _dot_nt(Q, Kc).reshape(hg, tq, w) * rg[:, None, :]
        if masked:
            ti = qi * tq + lax.broadcasted_iota(jnp.int32, (1, tq, w), 1)
            kj = k0 + lax.broadcasted_iota(jnp.int32, (1, tq, w), 2)
            s = jnp.where(kj <= ti, s, NEG)
        _softmax_update(s.reshape(hg * tq, w), Kc[:, :lora], m_sc, l_sc, acc_sc)

    nfull = (qi * tq) // ck

    def body(c, carry):
        chunk(c, False)
        return carry

    lax.fori_loop(0, nfull, body, 0)
    half = ck // 2
    if half % tq == 0:
        fits = (qi * tq - nfull * ck) + tq <= half

        @pl.when(fits)
        def _():
            chunk(nfull, True, half)

        @pl.when(jnp.logical_not(fits))
        def _():
            chunk(nfull, True)
    else:
        chunk(nfull, True)
    _attn_finish(o_ref, wuv_ref, l_sc, acc_sc, hg, tq, vd)


def prompt_attention(q4, kb3, r4, wuv, tq, ck, hg, name):
    heads, B, T, W = q4.shape
    Tk = kb3.shape[1]
    lora, vd = wuv.shape[1], wuv.shape[2]
    rows = hg * tq
    return pl.pallas_call(
        functools.partial(_pattn_kernel, hg=hg, tq=tq, ck=ck, lora=lora, vd=vd),
        grid=(B, T // tq, heads // hg),
        in_specs=[pl.BlockSpec((hg, 1, tq, W), lambda b, i, g: (g, b, i, 0)),
                  pl.BlockSpec((1, Tk, W), lambda b, i, g: (b, 0, 0)),
                  pl.BlockSpec((1, Tk // ck, heads, ck), lambda b, i, g: (b, 0, 0, 0)),
                  pl.BlockSpec((hg, lora, vd), lambda b, i, g: (g, 0, 0))],
        out_specs=pl.BlockSpec((1, tq, hg * vd), lambda b, i, g: (b, i, g)),
        out_shape=jax.ShapeDtypeStruct((B, T, heads * vd), F32),
        scratch_shapes=[pltpu.VMEM((rows, 1), F32), pltpu.VMEM((rows, 1), F32), pltpu.VMEM((rows, lora), F32)],
        compiler_params=_cp("parallel", "parallel", "arbitrary"),
        name=name,
    )(q4, kb3, r4, wuv)


def _sattn_kernel(pt_ref, q_ref, *refs, npp, nsteps, heads, tq, page, lora, vd, nope, qk, group, make_r):
    pages = refs[:npp]
    if make_r:
        wuk_ref, kn_ref, rn_ref, wuv_ref, o_ref, rp_ref, m_sc, l_sc, acc_sc, kt_sc = refs[npp:]
    else:
        rp_ref, kn_ref, rn_ref, wuv_ref, o_ref, m_sc, l_sc, acc_sc, kt_sc = refs[npp:]
    step = pl.program_id(1)

    @pl.when(step == 0)
    def _():
        _attn_init(m_sc, l_sc, acc_sc)

    Q = q_ref[0]
    ssr = []
    for i in range(npp):
        pg = pages[i][0]
        kt_sc[:, i * page:(i + 1) * page] = pg.astype(BF16)
        if make_r:
            kr = pg[lora:]
            ssr.append(jnp.sum(kr * kr, axis=0, keepdims=True))
    nk = npp * page
    if make_r:
        for g in range(0, npp, group):
            n0, n1 = g * page, (g + group) * page
            kn = _dot(wuk_ref[...], kt_sc[:lora, n0:n1])
            ssh = jnp.sum((kn * kn).reshape(heads, nope, n1 - n0), axis=1)
            rp_ref[0, :, n0:n1] = lax.rsqrt((ssh + jnp.concatenate(ssr[g:g + group], axis=1)) / qk + EPS)
    s = _dot(Q, kt_sc[...]).reshape(heads, tq, nk) * rp_ref[0][:, None, :]
    _softmax_update(s.reshape(heads * tq, nk), kt_sc[:lora, :], m_sc, l_sc, acc_sc, v_transposed=True)

    @pl.when(step == nsteps - 1)
    def _():
        Kn = kn_ref[0]
        s = _dot(Q, Kn).reshape(heads, tq, page) * rn_ref[0][:, None, :]
        ti = lax.broadcasted_iota(jnp.int32, (1, tq, page), 1)
        kj = lax.broadcasted_iota(jnp.int32, (1, tq, page), 2)
        s = jnp.where(kj <= ti, s, NEG)
        _softmax_update(s.reshape(heads * tq, page), Kn[:lora, :], m_sc, l_sc, acc_sc, v_transposed=True)
        _attn_finish(o_ref, wuv_ref, l_sc, acc_sc, heads, tq, vd)


def sample_attention(page_table, q3, cacheT, r_or_wuk, knewT, rnew, wuv, tq, npp, nope, qk, make_r, name):
    nseq, npages = page_table.shape
    _, W, page = cacheT.shape
    heads, lora, vd = wuv.shape
    nsteps = npages // npp
    group = 4 if npp % 4 == 0 else 1
    page_specs = [pl.BlockSpec((1, W, page), (lambda b, s, pt, i=i: (pt[b, s * npp + i], 0, 0)))
                  for i in range(npp)]
    r_spec = pl.BlockSpec((1, heads, npp * page), lambda b, s, pt: (b, 0, s))
    o_spec = pl.BlockSpec((1, tq, heads * vd), lambda b, s, pt: (b, 0, 0))
    o_shape = jax.ShapeDtypeStruct((nseq, tq, heads * vd), F32)
    if make_r:
        first = pl.BlockSpec(r_or_wuk.shape, lambda b, s, pt: (0, 0))
        out_specs = [o_spec, r_spec]
        out_shape = [o_shape, jax.ShapeDtypeStruct((nseq, heads, npages * page), F32)]
    else:
        first = r_spec
        out_specs, out_shape = o_spec, o_shape
    return pl.pallas_call(
        functools.partial(_sattn_kernel, npp=npp, nsteps=nsteps, heads=heads, tq=tq, page=page,
                          lora=lora, vd=vd, nope=nope, qk=qk, group=group, make_r=make_r),
        grid_spec=pltpu.PrefetchScalarGridSpec(
            num_scalar_prefetch=1,
            grid=(nseq, nsteps),
            in_specs=[pl.BlockSpec((1, heads * tq, W), lambda b, s, pt: (b, 0, 0))] + page_specs + [
                first,
                pl.BlockSpec((1, W, page), lambda b, s, pt: (b, 0, 0)),
                pl.BlockSpec((1, heads, page), lambda b, s, pt: (b, 0, 0)),
                pl.BlockSpec((heads, lora, vd), lambda b, s, pt: (0, 0, 0))],
            out_specs=out_specs,
            scratch_shapes=[pltpu.VMEM((heads * tq, 1), F32), pltpu.VMEM((heads * tq, 1), F32),
                            pltpu.VMEM((heads * tq, lora), F32), pltpu.VMEM((W, npp * page), BF16)],
        ),
        out_shape=out_shape,
        compiler_params=_cp("parallel", "arbitrary"),
        name=name,
    )(page_table, q3, *([cacheT] * npp), r_or_wuk, knewT, rnew, wuv)


def _rope_table(pos, rope):
    inv = 1.0 / (ROPE_BASE ** (jnp.arange(0, rope, 2, dtype=F32) / rope))
    f = pos.astype(F32)[:, None] * inv[None, :]
    emb = jnp.concatenate([f, f], -1)
    return jnp.concatenate([jnp.cos(emb), jnp.sin(emb)], -1)


def _rot_cols(w, rope):
    h = rope // 2
    return jnp.concatenate([-w[..., h:], w[..., :h]], -1)


def kernel(x_prompt, x_sample, state_ssm, state_conv, cache_kv, page_table, meta_tokens,
           a_norm, a_w_in, a_conv_w, a_conv_b, a_dt_bias, a_A_log, a_D, a_gate_norm, a_w_out,
           kv_norm, w_kv_a, kv_a_norm, w_uk, w_uv, k_norm,
           b_norm, b_w_in, b_q_a_norm, b_w_q, b_q_norm, b_w_out):
    bt, L, dm = x_prompt.shape
    nseq, S, _ = x_sample.shape
    n_meta = meta_tokens.shape[0]
    n_a = a_w_in.shape[0]
    n_b = b_w_in.shape[0]
    _, _, H, hd, N = state_ssm.shape
    cw, cdim = a_conv_w.shape[1:]
    d_inner = a_w_out.shape[1]
    G = (cdim - d_inner) // (2 * N)
    sdims = (H, hd, G, N)
    lora, heads, nope = w_uk.shape
    vd = w_uv.shape[2]
    kvrow = cache_kv.shape[2]
    rope = kvrow - lora
    qk = nope + rope
    qlora = b_q_a_norm.shape[1]
    page = cache_kv.shape[1]
    past = page_table.shape[1] * page
    tile = SUBLANES
    assert cw - 1 + S + 1 == tile and CHUNK % tile == 0 and (nseq * tile) % CHUNK == 0

    T = L + n_meta
    Tp = -(-T // CHUNK) * CHUNK
    Mp = bt * Tp
    Ms = nseq * S
    tmp = _pick(Mp, 512)
    tms = _pick(Ms, 512)

    xp = jnp.concatenate([jnp.broadcast_to(meta_tokens[None], (bt, n_meta, dm)), x_prompt,
                          jnp.zeros((bt, Tp - T, dm), F32)], 1).reshape(Mp, dm)
    xs = x_sample.reshape(Ms, dm)

    ssm_p, conv_p, conv_s = [], [], []
    s_new = None
    a_w_inT = jnp.transpose(a_w_in, (0, 2, 1))
    a_w_inT16 = a_w_inT.astype(BF16)
    for i in range(n_a):
        w_out = a_w_out[i].astype(BF16)
        A = -jnp.exp(a_A_log[i].astype(F32))
        tn = _pick(d_inner + cdim, 1024)
        tmw = _pick(Mp, 1088)

        proj = norm_matmul_t(xp, a_norm[i], a_w_inT16, i, d_inner + cdim, tmw, tn, name=f"a{i}_in_p")
        dt = dt_proj(xp, a_norm[i], a_w_inT, i, d_inner + cdim, a_dt_bias[i], tmp, name=f"a{i}_dt_p")
        proj3 = proj.reshape(bt, Tp, d_inner + cdim)
        act = conv_silu(proj3, d_inner, a_conv_w[i], a_conv_b[i], 256,True, name=f"a{i}_conv_p")
        y, st = ssd_chunks(proj3, act, dt.T, A, a_D[i], a_gate_norm[i], sdims,
                           seg=CHUNK, lo=0, hi=T, tile=None, name=f"a{i}_ssd_p")
        xp = matmul_res(y.reshape(Mp, d_inner), w_out, xp, tmw, 512, d_inner, name=f"a{i}_out_p")
        conv_p.append(proj3[:, T - (cw - 1):T, d_inner:])
        ssm_p.append(st.reshape(bt, G, N, H // G, hd).transpose(0, 1, 3, 4, 2).reshape(bt, H, hd, N))

        proj_s = norm_matmul_t(xs, a_norm[i], a_w_inT16, i, d_inner + cdim, tms, tn, name=f"a{i}_in_s")
        dt_s = dt_proj(xs, a_norm[i], a_w_inT, i, d_inner + cdim, a_dt_bias[i], tms, name=f"a{i}_dt_s")
        ps3 = proj_s.reshape(nseq, S, d_inner + cdim)
        z8 = jnp.concatenate([jnp.zeros((nseq, tile - S, d_inner), F32), ps3[..., :d_inner]], 1)
        full8 = jnp.concatenate([jnp.zeros((nseq, 1, cdim), F32), state_conv[i], ps3[..., d_inner:]], 1)
        conv_s.append(full8[:, tile - (cw - 1):])
        dt8T = jnp.concatenate([jnp.zeros((nseq, tile - S, H), F32), dt_s.reshape(nseq, S, H)],
                               1).reshape(nseq * tile, H).T
        act8 = conv_silu(full8.reshape(1, nseq * tile, cdim), 0, a_conv_w[i], a_conv_b[i], 256,False,
                         name=f"a{i}_conv_s")
        ys, s_new = sample_state(state_ssm.reshape(n_a, nseq, d_inner, N), i, act8[0], dt8T, A, sdims,
                                 tile=tile, lo=tile - S, prev=s_new, name=f"a{i}_state_s")
        nblk = nseq * tile // CHUNK
        y8 = ssd_chunks(z8.reshape(nblk, CHUNK, d_inner), act8.reshape(nblk, CHUNK, cdim), dt8T, A, a_D[i],
                        a_gate_norm[i], sdims, seg=tile, lo=tile - S, hi=tile, tile=tile,
                        ystate=ys.reshape(nblk, CHUNK, d_inner), name=f"a{i}_ssd_s")
        y_s = y8.reshape(nseq, tile, d_inner)[:, tile - S:].reshape(Ms, d_inner)
        xs = matmul_res(y_s, w_out, xs, tms, dm, 512, name=f"a{i}_out_s")

    wa = lora + LANES
    w_kv_ext = jnp.concatenate([w_kv_a, _rot_cols(w_kv_a[:, lora:], rope)], 1).astype(BF16)
    cs_p = jnp.tile(_rope_table(jnp.arange(Tp), rope), (bt, 1))
    cs_s = jnp.tile(_rope_table(past + jnp.arange(S), rope), (nseq, 1))
    wukT2 = w_uk.transpose(1, 2, 0).reshape(heads * nope, lora)
    waug = jnp.zeros((heads * nope + rope, wa), F32)
    waug = waug.at[:heads * nope, :lora].set(wukT2)
    waug = waug.at[heads * nope:, lora:lora + rope].set(jnp.eye(rope, dtype=F32)).astype(BF16)
    wukT = w_uk.transpose(1, 2, 0).astype(BF16)
    wuv = w_uv.transpose(1, 0, 2).astype(BF16)

    a_p = norm_matmul(xp, kv_norm, w_kv_ext, tmp, wa, name="kv_a_p")
    rows_p, kb_p = kv_post(a_p, kv_a_norm, cs_p, lora, rope, tmp, name="kv_post_p")
    rT_p = key_rms(kb_p, waug, heads, nope, qk, _pick(Mp, 256), name="kv_rms_p")
    a_s = norm_matmul(xs, kv_norm, w_kv_ext, tms, wa, name="kv_a_s")
    rows_s, kb_s = kv_post(a_s, kv_a_norm, cs_s, lora, rope, tms, name="kv_post_s")
    rT_s = key_rms(kb_s, waug, heads, nope, qk, _pick(Ms, 256), name="kv_rms_s")
    npp = min(32, page_table.shape[1])
    cacheT = jnp.transpose(cache_kv, (0, 2, 1))
    r_past = None

    ck = 4 * CHUNK
    Tk = -(-Tp // ck) * ck
    kb3 = jnp.pad(kb_p.reshape(bt, Tp, wa), ((0, 0), (0, Tk - Tp), (0, 0)))
    r4 = jnp.pad(rT_p.reshape(heads, bt, Tp), ((0, 0), (0, 0), (0, Tk - Tp)), constant_values=1.0)
    r4 = r4.reshape(heads, bt, Tk // ck, ck).transpose(1, 2, 0, 3)
    knew = jnp.pad(kb_s[:, :kvrow].reshape(nseq, S, kvrow), ((0, 0), (0, page - S), (0, 0))).transpose(0, 2, 1)
    rnew = jnp.pad(rT_s.reshape(heads, nseq, S).transpose(1, 0, 2), ((0, 0), (0, 0), (0, page - S)),
                   constant_values=1.0)

    def ext(v):
        return jnp.pad(v.astype(F32), (0, LANES - rope)).reshape(1, nope + LANES)

    qdims = (heads, nope, rope, lora, qk)
    tq = CHUNK
    for j in range(n_b):
        w_in = b_w_in[j].astype(BF16)
        wq3 = b_w_q[j].reshape(qlora, heads, qk)
        wq_ext = jnp.concatenate([wq3, _rot_cols(wq3[..., nope:], rope)], -1).reshape(qlora, heads * (nope + LANES))
        wq_ext = wq_ext.astype(BF16)
        w_out = b_w_out[j].astype(BF16)
        tn = _pick(w_in.shape[1], 1280)

        tmw = _pick(Mp, 1088)
        proj = norm_matmul(xp, b_norm[j], w_in, tmw, _pick(w_in.shape[1], 640), name=f"b{j}_in_p")
        q = q_side(proj, b_q_a_norm[j], wq_ext, cs_p, ext(b_q_norm[j]), ext(k_norm), wukT, qdims,
                   _pick(Mp, 256), name=f"b{j}_q_p")
        o = prompt_attention(q.reshape(heads, bt, Tp, wa), kb3, r4, wuv, tq, ck, min(16, heads),
                             name=f"b{j}_attn_p")
        xp = matmul_res(o.reshape(Mp, heads * vd), w_out, xp, tmp, dm, 512, name=f"b{j}_out_p",
                        gate=proj, gate_col0=qlora)

        proj_s = norm_matmul(xs, b_norm[j], w_in, tms, tn, name=f"b{j}_in_s")
        q_s = q_side(proj_s, b_q_a_norm[j], wq_ext, cs_s, ext(b_q_norm[j]), ext(k_norm), wukT, qdims,
                     _pick(Ms, 256), name=f"b{j}_q_s")
        q8 = jnp.pad(q_s[..., :kvrow].reshape(heads, nseq, S, kvrow), ((0, 0), (0, 0), (0, tile - S), (0, 0)))
        q8 = q8.transpose(1, 0, 2, 3).reshape(nseq, heads * tile, kvrow)
        if r_past is None:
            o_s, r_past = sample_attention(page_table, q8, cacheT, wukT2.astype(BF16), knew, rnew, wuv, tile, npp,
                                           nope, qk, True, name=f"b{j}_attn_s")
        else:
            o_s = sample_attention(page_table, q8, cacheT, r_past, knew, rnew, wuv, tile, npp,
                                   nope, qk, False, name=f"b{j}_attn_s")
        xs = matmul_res(o_s[:, :S].reshape(Ms, heads * vd), w_out, xs, tms, dm, 512, name=f"b{j}_out_s",
                        gate=proj_s, gate_col0=qlora)

    y_prompt = xp.reshape(bt, Tp, dm)[:, n_meta:T]
    kv_p = rows_p.reshape(bt, Tp, kvrow)[:, :T]
    return (y_prompt, xs.reshape(nseq, S, dm), jnp.stack(ssm_p), jnp.stack(conv_p), kv_p,
            s_new.reshape(n_a, nseq, H, hd, N), jnp.stack(conv_s), rows_s.reshape(nseq, S, kvrow))
```
